```python
import math
import jax, jax.numpy as jnp
from jax import lax
import numpy as np

D_MODEL = 1024
BATCH = 4
SEQ = 8192
DEPTH = 4

N_MIXERS = 3
N_A = (DEPTH + 2) // 3
N_B = (DEPTH + 1) // 3
N_C = DEPTH // 3

CONV_KERNEL = 31

DA_HEADS = 8
DA_HEAD_DIM = 64
Q_BLOCK = 128

GMLP_FFN = 6 * D_MODEL
GMLP_HALF = GMLP_FFN // 2
GMLP_GROUPS = 4
CHUNK = 128

FFN_DIM = ((8 * D_MODEL // 3 + 127) // 128) * 128
FFN_CONV = 3

ALPHA = (2 * DEPTH) ** 0.25
BETA = (8 * DEPTH) ** -0.25
LN_EPS = 1e-5
RMS_EPS = 1e-5

kernel_name = "hybrid_conv_diffattn_gmlp_deepnorm"


def layer_norm(x, g, b):
    xf = x.astype(jnp.float32)
    mu = jnp.mean(xf, axis=-1, keepdims=True)
    var = jnp.mean(jnp.square(xf - mu), axis=-1, keepdims=True)
    y = (xf - mu) * lax.rsqrt(var + LN_EPS) * g.astype(jnp.float32) + b.astype(jnp.float32)
    return y.astype(x.dtype)


def causal_dwconv(x, w, b):
    k_width, ch = w.shape
    y = lax.conv_general_dilated(
        x, w[:, None, :].astype(x.dtype), window_strides=(1,),
        padding=[(k_width - 1, 0)],
        dimension_numbers=("NWC", "WIO", "NWC"),
        feature_group_count=ch)
    return y + b


def conformer_conv(x, w_in, b_in, w_dw, b_dw, ln_g, ln_b, w_out, b_out):
    h = x @ w_in + b_in
    a, g = jnp.split(h, 2, axis=-1)
    h = a * jax.nn.sigmoid(g)
    h = causal_dwconv(h, w_dw, b_dw)
    h = jax.nn.silu(layer_norm(h, ln_g, ln_b))
    return h @ w_out + b_out


def diff_attention(x, w_qkv, lq1, lk1, lq2, lk2, subln_g, w_o, lambda_init):
    bsz, seq, _ = x.shape
    f32 = jnp.float32
    qkv = x @ w_qkv
    q, k, v = jnp.split(qkv, 3, axis=-1)
    q = q.reshape(bsz, seq, DA_HEADS, 2, DA_HEAD_DIM)
    k = k.reshape(bsz, seq, DA_HEADS, 2, DA_HEAD_DIM)
    v = v.reshape(bsz, seq, DA_HEADS, 2 * DA_HEAD_DIM)
    lam = (jnp.exp(jnp.sum(lq1.astype(f32) * lk1.astype(f32)))
           - jnp.exp(jnp.sum(lq2.astype(f32) * lk2.astype(f32))) + lambda_init)
    n_blk = seq // Q_BLOCK
    q_blocks = jnp.moveaxis(
        q.reshape(bsz, n_blk, Q_BLOCK, DA_HEADS, 2, DA_HEAD_DIM), 1, 0)
    k_pos = jnp.arange(seq)
    scale = DA_HEAD_DIM ** -0.5

    def attend(args):
        qb, blk = args
        s = jnp.einsum("bqhcd,bkhcd->bhcqk", qb, k,
                       preferred_element_type=f32) * scale
        q_pos = blk * Q_BLOCK + jnp.arange(Q_BLOCK)
        mask = k_pos[None, :] <= q_pos[:, None]
        p = jax.nn.softmax(jnp.where(mask, s, -jnp.inf), axis=-1)
        a = p[:, :, 0] - lam * p[:, :, 1]
        return jnp.einsum("bhqk,bkhe->bqhe", a.astype(v.dtype), v)

    o = lax.map(attend, (q_blocks, jnp.arange(n_blk)))
    o = jnp.moveaxis(o, 0, 1).reshape(bsz, seq, DA_HEADS, 2 * DA_HEAD_DIM)
    of = o.astype(f32)
    of = (of * lax.rsqrt(jnp.mean(of * of, axis=-1, keepdims=True) + RMS_EPS)
          * subln_g.astype(f32) * (1.0 - lambda_init))
    return of.astype(x.dtype).reshape(bsz, seq, DA_HEADS * 2 * DA_HEAD_DIM) @ w_o


def chunked_gmlp(x, w_in, b_in, ln_g, ln_b, w_s, b_s, w_out, b_out):
    bsz, seq, _ = x.shape
    z = jax.nn.gelu(x @ w_in + b_in, approximate=False)
    u, v = jnp.split(z, 2, axis=-1)
    v = layer_norm(v, ln_g, ln_b)
    v = v.reshape(bsz, seq // CHUNK, CHUNK, GMLP_GROUPS, GMLP_HALF // GMLP_GROUPS)
    tril = jnp.tril(jnp.ones((CHUNK, CHUNK), dtype=bool))
    w = jnp.where(tril[None], w_s, jnp.zeros_like(w_s))
    sv = jnp.einsum("gts,bnsgc->bntgc", w, v) + b_s.T[:, :, None]
    out = u * sv.reshape(bsz, seq, GMLP_HALF)
    return out @ w_out + b_out


def conv_ffn(x, w_up, b_up, w_dw, b_dw, w_down, b_down):
    h = x @ w_up + b_up
    h = causal_dwconv(h, w_dw, b_dw)
    g, val = jnp.split(h, 2, axis=-1)
    return (jax.nn.silu(g) * val) @ w_down + b_down


def _normal(k, shape, scale):
    return jax.random.normal(k, shape, jnp.float32) * scale


def setup_inputs(seed: int = 0) -> dict:
    key = jax.random.key(seed)
    ks = iter(jax.random.split(key, 48))
    D = D_MODEL
    F = FFN_DIM
    qkv_w = 3 * DA_HEADS * 2 * DA_HEAD_DIM
    attn_w = DA_HEADS * 2 * DA_HEAD_DIM
    return {
        "x": _normal(next(ks), (BATCH, SEQ, D), 1.0),
        "a_w_in": _normal(next(ks), (N_A, D, 2 * D), D ** -0.5),
        "a_b_in": _normal(next(ks), (N_A, 2 * D), 0.01),
        "a_w_dw": _normal(next(ks), (N_A, CONV_KERNEL, D), CONV_KERNEL ** -0.5),
        "a_b_dw": _normal(next(ks), (N_A, D), 0.01),
        "a_ln_g": 1.0 + _normal(next(ks), (N_A, D), 0.01),
        "a_ln_b": _normal(next(ks), (N_A, D), 0.01),
        "a_w_out": _normal(next(ks), (N_A, D, D), BETA * D ** -0.5),
        "a_b_out": _normal(next(ks), (N_A, D), 0.01),
        "b_w_qkv": _normal(next(ks), (N_B, D, qkv_w), D ** -0.5),
        "b_lq1": _normal(next(ks), (N_B, DA_HEAD_DIM), 0.1),
        "b_lk1": _normal(next(ks), (N_B, DA_HEAD_DIM), 0.1),
        "b_lq2": _normal(next(ks), (N_B, DA_HEAD_DIM), 0.1),
        "b_lk2": _normal(next(ks), (N_B, DA_HEAD_DIM), 0.1),
        "b_subln_g": 1.0 + _normal(next(ks), (N_B, 2 * DA_HEAD_DIM), 0.01),
        "b_w_o": _normal(next(ks), (N_B, attn_w, D), BETA * attn_w ** -0.5),
        "c_w_in": _normal(next(ks), (N_C, D, GMLP_FFN), D ** -0.5),
        "c_b_in": _normal(next(ks), (N_C, GMLP_FFN), 0.01),
        "c_ln_g": 1.0 + _normal(next(ks), (N_C, GMLP_HALF), 0.01),
        "c_ln_b": _normal(next(ks), (N_C, GMLP_HALF), 0.01),
        "c_w_s": _normal(next(ks), (N_C, GMLP_GROUPS, CHUNK, CHUNK), CHUNK ** -0.5),
        "c_b_s": 1.0 + _normal(next(ks), (N_C, GMLP_GROUPS, CHUNK), 0.01),
        "c_w_out": _normal(next(ks), (N_C, GMLP_HALF, D), BETA * GMLP_HALF ** -0.5),
        "c_b_out": _normal(next(ks), (N_C, D), 0.01),
        "f_w_up": _normal(next(ks), (DEPTH, D, 2 * F), D ** -0.5),
        "f_b_up": _normal(next(ks), (DEPTH, 2 * F), 0.01),
        "f_w_dw": _normal(next(ks), (DEPTH, FFN_CONV, 2 * F), FFN_CONV ** -0.5),
        "f_b_dw": _normal(next(ks), (DEPTH, 2 * F), 0.01),
        "f_w_down": _normal(next(ks), (DEPTH, F, D), BETA * F ** -0.5),
        "f_b_down": _normal(next(ks), (DEPTH, D), 0.01),
        "ln_mix_g": 1.0 + _normal(next(ks), (DEPTH, D), 0.01),
        "ln_mix_b": _normal(next(ks), (DEPTH, D), 0.01),
        "ln_ffn_g": 1.0 + _normal(next(ks), (DEPTH, D), 0.01),
        "ln_ffn_b": _normal(next(ks), (DEPTH, D), 0.01),
    }


def reference(x,
              a_w_in, a_b_in, a_w_dw, a_b_dw, a_ln_g, a_ln_b, a_w_out, a_b_out,
              b_w_qkv, b_lq1, b_lk1, b_lq2, b_lk2, b_subln_g, b_w_o,
              c_w_in, c_b_in, c_ln_g, c_ln_b, c_w_s, c_b_s, c_w_out, c_b_out,
              f_w_up, f_b_up, f_w_dw, f_b_dw, f_w_down, f_b_down,
              ln_mix_g, ln_mix_b, ln_ffn_g, ln_ffn_b):
    for i in range(DEPTH):
        kind = i % N_MIXERS
        j = i // N_MIXERS
        if kind == 0:
            y = conformer_conv(x, a_w_in[j], a_b_in[j], a_w_dw[j], a_b_dw[j],
                               a_ln_g[j], a_ln_b[j], a_w_out[j], a_b_out[j])
        elif kind == 1:
            lambda_init = 0.8 - 0.6 * math.exp(-0.3 * i)
            y = diff_attention(x, b_w_qkv[j], b_lq1[j], b_lk1[j], b_lq2[j], b_lk2[j],
                               b_subln_g[j], b_w_o[j], lambda_init)
        else:
            y = chunked_gmlp(x, c_w_in[j], c_b_in[j], c_ln_g[j], c_ln_b[j],
                             c_w_s[j], c_b_s[j], c_w_out[j], c_b_out[j])
        x = layer_norm(ALPHA * x + y, ln_mix_g[i], ln_mix_b[i])
        f = conv_ffn(x, f_w_up[i], f_b_up[i], f_w_dw[i], f_b_dw[i],
                     f_w_down[i], f_b_down[i])
        x = layer_norm(ALPHA * x + f, ln_ffn_g[i], ln_ffn_b[i])
    return x
```

```python
import functools
import math

import jax
import jax.numpy as jnp
from jax import lax
from jax.experimental import pallas as pl
from jax.experimental.pallas import tpu as pltpu

F32 = jnp.float32
BF16 = jnp.bfloat16

N_MIXERS = 3
CONV_KERNEL = 31
DA_HEADS = 8
DA_HEAD_DIM = 64
HEAD_W = 2 * DA_HEAD_DIM
GMLP_GROUPS = 4
CHUNK = 128
FFN_CONV = 3
LN_EPS = 1e-5
RMS_EPS = 1e-5

V7X_SUBLANES = 8
V7X_MXU_COLS = 256
V7X_VMEM_BYTES = 64 * 1024 * 1024
VMEM_LIMIT = V7X_VMEM_BYTES - 8 * 1024 * 1024

ROW_TILE = 512
CONV_TILE = 256
CONV_HALO = 32
GMLP_TILE = 256
ATT_TILE = 256
COL_CHUNK = V7X_MXU_COLS


def _params(n_grid):
    return pltpu.CompilerParams(
        dimension_semantics=("arbitrary",) * n_grid, vmem_limit_bytes=VMEM_LIMIT)


def _const_spec(shape):
    return pl.BlockSpec(shape, lambda *_: (0,) * len(shape), pipeline_mode=pl.Buffered(1))


def _layer_norm(r, g, b):
    mu = jnp.mean(r, axis=-1, keepdims=True)
    c = r - mu
    var = jnp.mean(c * c, axis=-1, keepdims=True)
    return c * lax.rsqrt(var + LN_EPS) * g + b


def _ffn_kernel(x_ref, wg_ref, wv_ref, bg_ref, bv_ref, cwg_ref, cwv_ref, cbg_ref, cbv_ref,
                wd_ref, bd_ref, lg_ref, lb_ref, o_ref, acc_ref, carry_g_ref, carry_v_ref,
                *, tiles_per_seq, n_chunks, alpha):
    first = (pl.program_id(0) % tiles_per_seq) == 0
    x = x_ref[...]
    xb = x.astype(BF16)
    rows, width = x.shape[0], wg_ref.shape[2]
    row = lax.broadcasted_iota(jnp.int32, (rows, width), 0)
    acc_ref[...] = jnp.zeros_like(acc_ref)

    def conv_half(j, w_ref, b_ref, cw_ref, cb_ref, carry_ref):
        h = jnp.dot(xb, w_ref[j], preferred_element_type=F32) + b_ref[j]
        tail = jnp.where(first, 0.0, carry_ref[j])
        prev1, prev2 = tail[V7X_SUBLANES - 1:], tail[V7X_SUBLANES - 2:V7X_SUBLANES - 1]
        h1 = jnp.where(row == 0, prev1, pltpu.roll(h, 1, 0))
        h2 = jnp.where(row == 0, prev2, jnp.where(row == 1, prev1, pltpu.roll(h, 2, 0)))
        carry_ref[j] = h[rows - V7X_SUBLANES:]
        cw = cw_ref[j]
        return cw[0:1] * h2 + cw[1:2] * h1 + cw[2:3] * h + cb_ref[j]

    def chunk(j, carry):
        g = conv_half(j, wg_ref, bg_ref, cwg_ref, cbg_ref, carry_g_ref)
        v = conv_half(j, wv_ref, bv_ref, cwv_ref, cbv_ref, carry_v_ref)
        act = (g * jax.nn.sigmoid(g)) * v
        acc_ref[...] += jnp.dot(act.astype(BF16), wd_ref[j], preferred_element_type=F32)
        return carry

    lax.fori_loop(0, n_chunks, chunk, 0)
    r = alpha * x + (acc_ref[...] + bd_ref[...])
    o_ref[...] = _layer_norm(r, lg_ref[...], lb_ref[...])


def _conv_ffn(x, seq, w_up, b_up, w_dw, b_dw, w_down, b_down, ln_g, ln_b, alpha):
    n, d = x.shape
    f = w_down.shape[0]
    tf = COL_CHUNK
    nc = f // tf
    assert nc * tf == f and n % ROW_TILE == 0 and seq % ROW_TILE == 0

    def cols(a):
        return a.reshape(a.shape[0], nc, tf).transpose(1, 0, 2)

    wg, wv = cols(w_up[:, :f].astype(BF16)), cols(w_up[:, f:].astype(BF16))
    bg, bv = cols(b_up[None, :f]), cols(b_up[None, f:])
    cwg, cwv = cols(w_dw[:, :f]), cols(w_dw[:, f:])
    cbg, cbv = cols(b_dw[None, :f]), cols(b_dw[None, f:])
    wd = w_down.astype(BF16).reshape(nc, tf, d)
    row_spec = pl.BlockSpec((ROW_TILE, d), lambda i: (i, 0))
    kern = functools.partial(_ffn_kernel, tiles_per_seq=seq // ROW_TILE, n_chunks=nc, alpha=alpha)
    return pl.pallas_call(
        kern,
        grid=(n // ROW_TILE,),
        in_specs=[row_spec] + [_const_spec(a.shape) for a in (wg, wv, bg, bv, cwg, cwv, cbg, cbv, wd)]
        + [_const_spec((1, d))] * 3,
        out_specs=row_spec,
        out_shape=jax.ShapeDtypeStruct((n, d), F32),
        scratch_shapes=[pltpu.VMEM((ROW_TILE, d), F32),
                        pltpu.VMEM((nc, V7X_SUBLANES, tf), F32),
                        pltpu.VMEM((nc, V7X_SUBLANES, tf), F32)],
        compiler_params=_params(1),
        name="conv_ffn",
    )(x, wg, wv, bg, bv, cwg, cwv, cbg, cbv, wd, b_down[None], ln_g[None], ln_b[None])


def _proj_ln_kernel(y_ref, x_ref, w_ref, b_ref, lg_ref, lb_ref, o_ref, *, alpha):
    y = jnp.dot(y_ref[...].astype(BF16), w_ref[...], preferred_element_type=F32) + b_ref[...]
    o_ref[...] = _layer_norm(alpha * x_ref[...] + y, lg_ref[...], lb_ref[...])


def _proj_ln(y, x, w, b, ln_g, ln_b, alpha):
    n, d = x.shape
    k = y.shape[1]
    return pl.pallas_call(
        functools.partial(_proj_ln_kernel, alpha=alpha),
        grid=(n // ROW_TILE,),
        in_specs=[pl.BlockSpec((ROW_TILE, k), lambda i: (i, 0)),
                  pl.BlockSpec((ROW_TILE, d), lambda i: (i, 0)),
                  _const_spec((k, d))] + [_const_spec((1, d))] * 3,
        out_specs=pl.BlockSpec((ROW_TILE, d), lambda i: (i, 0)),
        out_shape=jax.ShapeDtypeStruct((n, d), F32),
        compiler_params=_params(1),
        name="proj_ln",
    )(y, x, w.astype(BF16), b[None], ln_g[None], ln_b[None])


def _glu_kernel(x_ref, wa_ref, wg_ref, ba_ref, bg_ref, o_ref):
    xb = x_ref[...].astype(BF16)
    for c in range(0, o_ref.shape[1], COL_CHUNK):
        sl = slice(c, c + COL_CHUNK)
        a = jnp.dot(xb, wa_ref[:, sl], preferred_element_type=F32) + ba_ref[:, sl]
        g = jnp.dot(xb, wg_ref[:, sl], preferred_element_type=F32) + bg_ref[:, sl]
        o_ref[:, sl] = a * jax.nn.sigmoid(g)


def _glu(x, w_in, b_in):
    n, d = x.shape
    w = w_in.astype(BF16)
    row_spec = pl.BlockSpec((ROW_TILE, d), lambda i: (i, 0))
    return pl.pallas_call(
        _glu_kernel,
        grid=(n // ROW_TILE,),
        in_specs=[row_spec, _const_spec((d, d)), _const_spec((d, d)),
                  _const_spec((1, d)), _const_spec((1, d))],
        out_specs=row_spec,
        out_shape=jax.ShapeDtypeStruct((n, d), F32),
        compiler_params=_params(1),
        name="glu_in",
    )(x, w[:, :d], w[:, d:], b_in[None, :d], b_in[None, d:])


def _conv_out_kernel(h_ref, halo_ref, x_ref, cw_ref, cb_ref, ng_ref, nb_ref, w_ref, b_ref,
                     lg_ref, lb_ref, o_ref, buf_ref, conv_ref, *, tiles_per_seq, alpha):
    first = (pl.program_id(0) % tiles_per_seq) == 0
    rows = h_ref.shape[0]
    buf_ref[:CONV_HALO] = jnp.where(first, 0.0, halo_ref[...])
    buf_ref[CONV_HALO:] = h_ref[...]
    lead = CONV_HALO - (CONV_KERNEL - 1)
    for c in range(0, h_ref.shape[1], COL_CHUNK):
        sl = slice(c, c + COL_CHUNK)
        buf = buf_ref[:, sl]
        cw = cw_ref[:, sl]
        acc = jnp.zeros((rows, COL_CHUNK), F32) + cb_ref[:, sl]
        for r in range(V7X_SUBLANES):
            shifted = buf if r == 0 else pltpu.roll(buf, buf.shape[0] - r, 0)
            for k in range(CONV_KERNEL):
                off = lead + k
                if off % V7X_SUBLANES == r:
                    base = off - r
                    acc = acc + cw[k:k + 1] * shifted[base:base + rows]
        conv_ref[:, sl] = acc
    hn = _layer_norm(conv_ref[...], ng_ref[...], nb_ref[...])
    hn = hn * jax.nn.sigmoid(hn)
    y = jnp.dot(hn.astype(BF16), w_ref[...], preferred_element_type=F32) + b_ref[...]
    o_ref[...] = _layer_norm(alpha * x_ref[...] + y, lg_ref[...], lb_ref[...])


def _conv_out(h, x, seq, w_dw, b_dw, n_g, n_b, w_out, b_out, ln_g, ln_b, alpha):
    n, d = x.shape
    t = CONV_TILE
    per_halo = t // CONV_HALO
    row_spec = pl.BlockSpec((t, d), lambda i: (i, 0))
    halo_spec = pl.BlockSpec((CONV_HALO, d), lambda i: (jnp.maximum(i * per_halo - 1, 0), 0))
    kern = functools.partial(_conv_out_kernel, tiles_per_seq=seq // t, alpha=alpha)
    return pl.pallas_call(
        kern,
        grid=(n // t,),
        in_specs=[row_spec, halo_spec, row_spec, _const_spec((CONV_KERNEL, d))]
        + [_const_spec((1, d))] * 3 + [_const_spec((d, d))] + [_const_spec((1, d))] * 3,
        out_specs=row_spec,
        out_shape=jax.ShapeDtypeStruct((n, d), F32),
        scratch_shapes=[pltpu.VMEM((CONV_HALO + t, d), F32), pltpu.VMEM((t, d), F32)],
        compiler_params=_params(1),
        name="conv_out",
    )(h, h, x, w_dw, b_dw[None], n_g[None], n_b[None], w_out.astype(BF16), b_out[None],
      ln_g[None], ln_b[None])


def _qkv_kernel(x_ref, w_ref, q_ref, k_ref, v_ref, *, scale):
    xb = x_ref[...].astype(BF16)
    d = q_ref.shape[1]
    for idx, (ref, mul) in enumerate(((q_ref, scale), (k_ref, None), (v_ref, None))):
        for c in range(0, d, COL_CHUNK):
            y = jnp.dot(xb, w_ref[:, idx * d + c: idx * d + c + COL_CHUNK],
                        preferred_element_type=F32)
            if mul is not None:
                y = y * mul
            ref[:, c:c + COL_CHUNK] = y.astype(BF16)


def _qkv(x, w_qkv):
    n, d = x.shape
    row_spec = pl.BlockSpec((ROW_TILE, d), lambda i: (i, 0))
    out = jax.ShapeDtypeStruct((n, d), BF16)
    return pl.pallas_call(
        functools.partial(_qkv_kernel, scale=DA_HEAD_DIM ** -0.5),
        grid=(n // ROW_TILE,),
        in_specs=[row_spec, _const_spec((d, 3 * d))],
        out_specs=[row_spec] * 3,
        out_shape=[out] * 3,
        compiler_params=_params(1),
        name="qkv",
    )(x, w_qkv.astype(BF16))


def _attn_kernel(lam_ref, qt_ref, k_ref, vt_ref, g_ref, o_ref,
                 m_ref, l_ref, acc_ref, *, out_scale):
    qi = pl.program_id(2)
    tq = qt_ref.shape[-1]
    tk = vt_ref.shape[-1]
    qt = qt_ref[0, 0, 0]
    comp = lax.broadcasted_iota(jnp.int32, qt.shape, 0) < DA_HEAD_DIM
    qts = (jnp.where(comp, qt, jnp.zeros_like(qt)), jnp.where(comp, jnp.zeros_like(qt), qt))
    m_ref[...] = jnp.full_like(m_ref, -jnp.inf)
    l_ref[...] = jnp.zeros_like(l_ref)
    acc_ref[...] = jnp.zeros_like(acc_ref)

    def step(kj, masked):
        kb = k_ref[0, pl.ds(pl.multiple_of(kj * tk, tk), tk), :]
        vt = vt_ref[0, 0, kj]
        for c in range(2):
            s = jnp.dot(kb, qts[c], preferred_element_type=F32)
            if masked:
                key = lax.broadcasted_iota(jnp.int32, s.shape, 0)
                qry = lax.broadcasted_iota(jnp.int32, s.shape, 1)
                s = jnp.where(key <= qry, s, -jnp.inf)
            m_old = m_ref[c]
            m_new = jnp.maximum(m_old, jnp.max(s, axis=0, keepdims=True))
            a = jnp.exp(m_old - m_new)
            p = jnp.exp(s - m_new)
            l_ref[c] = a * l_ref[c] + jnp.sum(p, axis=0, keepdims=True)
            acc_ref[c] = a * acc_ref[c] + jnp.dot(vt, p.astype(BF16), preferred_element_type=F32)
            m_ref[c] = m_new

    def body(kj, carry):
        step(kj, False)
        return carry

    lax.fori_loop(0, qi, body, 0)
    step(qi, True)

    lam = lam_ref[0, 0]
    o = acc_ref[0] / l_ref[0] - lam * (acc_ref[1] / l_ref[1])
    o = o * lax.rsqrt(jnp.mean(o * o, axis=0, keepdims=True) + RMS_EPS) * g_ref[...] * out_scale
    o_ref[...] = o.T.astype(o_ref.dtype)


def _attention(q, k, v, lam, subln_g, bsz, seq, out_scale):
    n, d = q.shape
    t = ATT_TILE
    nb = seq // t
    def transposed(a):
        return a.reshape(bsz, nb, t, DA_HEADS, HEAD_W).transpose(0, 3, 1, 4, 2)
    qt, vt = transposed(q), transposed(v)
    k3 = k.reshape(bsz, seq, d)
    return pl.pallas_call(
        functools.partial(_attn_kernel, out_scale=out_scale),
        grid=(bsz, DA_HEADS, nb),
        in_specs=[pl.BlockSpec(memory_space=pltpu.SMEM),
                  pl.BlockSpec((1, 1, 1, HEAD_W, t), lambda b, h, i: (b, h, i, 0, 0)),
                  pl.BlockSpec((1, seq, HEAD_W), lambda b, h, i: (b, 0, h)),
                  pl.BlockSpec((1, 1, nb, HEAD_W, t), lambda b, h, i: (b, h, 0, 0, 0)),
                  pl.BlockSpec((HEAD_W, 1), lambda b, h, i: (0, 0))],
        out_specs=pl.BlockSpec((t, HEAD_W), lambda b, h, i: (b * nb + i, h)),
        out_shape=jax.ShapeDtypeStruct((n, d), BF16),
        scratch_shapes=[pltpu.VMEM((2, 1, t), F32), pltpu.VMEM((2, 1, t), F32),
                        pltpu.VMEM((2, HEAD_W, t), F32)],
        compiler_params=_params(3),
        name="diff_attention",
    )(lam.reshape(1, 1), qt, k3, vt, subln_g[:, None])


def _gelu_in_kernel(x_ref, w_ref, b_ref, o_ref):
    xb = x_ref[...].astype(BF16)
    for c in range(0, o_ref.shape[1], COL_CHUNK):
        sl = slice(c, c + COL_CHUNK)
        z = jnp.dot(xb, w_ref[:, sl], preferred_element_type=F32) + b_ref[:, sl]
        o_ref[:, sl] = 0.5 * z * (1.0 + lax.erf(z * math.sqrt(0.5)))


def _gelu_in(x, w_in, b_in, tn):
    n, d = x.shape
    f = w_in.shape[1]
    return pl.pallas_call(
        _gelu_in_kernel,
        grid=(n // ROW_TILE, f // tn),
        in_specs=[pl.BlockSpec((ROW_TILE, d), lambda i, j: (i, 0)),
                  pl.BlockSpec((d, tn), lambda i, j: (0, j)),
                  pl.BlockSpec((1, tn), lambda i, j: (0, j))],
        out_specs=pl.BlockSpec((ROW_TILE, tn), lambda i, j: (i, j)),
        out_shape=jax.ShapeDtypeStruct((n, f), F32),
        compiler_params=_params(2),
        name="gelu_in",
    )(x, w_in.astype(BF16), b_in[None])


def _gate_out_kernel(u_ref, v_ref, x_ref, ng_ref, nb_ref, ws_ref, bs_ref, w_ref, b_ref,
                     lg_ref, lb_ref, o_ref, gated_ref, *, alpha):
    rows, half = u_ref.shape
    gw = half // GMLP_GROUPS
    vn = _layer_norm(v_ref[...], ng_ref[...], nb_ref[...]).astype(BF16)
    tri_r = lax.broadcasted_iota(jnp.int32, (CHUNK, CHUNK), 0)
    tri_c = lax.broadcasted_iota(jnp.int32, (CHUNK, CHUNK), 1)
    for g in range(GMLP_GROUPS):
        ws = jnp.where(tri_c <= tri_r, ws_ref[g], 0.0).astype(BF16)
        for r in range(0, rows, CHUNK):
            cs = slice(g * gw, (g + 1) * gw)
            sv = jnp.dot(ws, vn[r:r + CHUNK, cs], preferred_element_type=F32) + bs_ref[g]
            gated_ref[r:r + CHUNK, cs] = (u_ref[r:r + CHUNK, cs] * sv).astype(BF16)
    y = jnp.dot(gated_ref[...], w_ref[...], preferred_element_type=F32) + b_ref[...]
    o_ref[...] = _layer_norm(alpha * x_ref[...] + y, lg_ref[...], lb_ref[...])


def _gate_out(z, x, n_g, n_b, w_s, b_s, w_out, b_out, ln_g, ln_b, alpha):
    n, d = x.shape
    half = z.shape[1] // 2
    t = GMLP_TILE
    return pl.pallas_call(
        functools.partial(_gate_out_kernel, alpha=alpha),
        grid=(n // t,),
        in_specs=[pl.BlockSpec((t, half), lambda i: (i, 0)),
                  pl.BlockSpec((t, half), lambda i: (i, 1)),
                  pl.BlockSpec((t, d), lambda i: (i, 0)),
                  _const_spec((1, half)), _const_spec((1, half)),
                  _const_spec((GMLP_GROUPS, CHUNK, CHUNK)), _const_spec((GMLP_GROUPS, CHUNK, 1)),
                  _const_spec((half, d))] + [_const_spec((1, d))] * 3,
        out_specs=pl.BlockSpec((t, d), lambda i: (i, 0)),
        out_shape=jax.ShapeDtypeStruct((n, d), F32),
        scratch_shapes=[pltpu.VMEM((t, half), BF16)],
        compiler_params=_params(1),
        name="gate_out",
    )(z, z, x, n_g[None], n_b[None], w_s, b_s[:, :, None], w_out.astype(BF16), b_out[None],
      ln_g[None], ln_b[None])


def kernel(x, a_w_in, a_b_in, a_w_dw, a_b_dw, a_ln_g, a_ln_b, a_w_out, a_b_out, b_w_qkv, b_lq1, b_lk1, b_lq2, b_lk2, b_subln_g, b_w_o, c_w_in, c_b_in, c_ln_g, c_ln_b, c_w_s, c_b_s, c_w_out, c_b_out, f_w_up, f_b_up, f_w_dw, f_b_dw, f_w_down, f_b_down, ln_mix_g, ln_mix_b, ln_ffn_g, ln_ffn_b):
    bsz, seq, d = x.shape
    depth = f_w_up.shape[0]
    alpha = (2 * depth) ** 0.25
    x = x.reshape(bsz * seq, d)
    for i in range(depth):
        kind, j = i % N_MIXERS, i // N_MIXERS
        mix_ln = (ln_mix_g[i], ln_mix_b[i], alpha)
        if kind == 0:
            h = _glu(x, a_w_in[j], a_b_in[j])
            x = _conv_out(h, x, seq, a_w_dw[j], a_b_dw[j], a_ln_g[j], a_ln_b[j],
                          a_w_out[j], a_b_out[j], *mix_ln)
        elif kind == 1:
            lambda_init = 0.8 - 0.6 * math.exp(-0.3 * i)
            lam = (jnp.exp(jnp.sum(b_lq1[j] * b_lk1[j])) - jnp.exp(jnp.sum(b_lq2[j] * b_lk2[j]))
                   + lambda_init)
            q, k, v = _qkv(x, b_w_qkv[j])
            o = _attention(q, k, v, lam, b_subln_g[j], bsz, seq, 1.0 - lambda_init)
            x = _proj_ln(o, x, b_w_o[j], jnp.zeros((d,), F32), *mix_ln)
        else:
            z = _gelu_in(x, c_w_in[j], c_b_in[j], 1024)
            x = _gate_out(z, x, c_ln_g[j], c_ln_b[j], c_w_s[j], c_b_s[j],
                          c_w_out[j], c_b_out[j], *mix_ln)
        x = _conv_ffn(x, seq, f_w_up[i], f_b_up[i], f_w_dw[i], f_b_dw[i],
                      f_w_down[i], f_b_down[i], ln_ffn_g[i], ln_ffn_b[i], alpha)
    return x.reshape(bsz, seq, d)
```

```python
import functools
import math

import jax
import jax.numpy as jnp
from jax import lax
from jax.experimental import pallas as pl
from jax.experimental.pallas import tpu as pltpu

F32 = jnp.float32
BF16 = jnp.bfloat16

N_MIXERS = 3
CONV_KERNEL = 31
DA_HEADS = 8
DA_HEAD_DIM = 64
HEAD_W = 2 * DA_HEAD_DIM
GMLP_GROUPS = 4
CHUNK = 128
FFN_CONV = 3
LN_EPS = 1e-5
RMS_EPS = 1e-5

V7X_SUBLANES = 8
V7X_MXU_COLS = 256
V7X_VMEM_BYTES = 64 * 1024 * 1024
VMEM_LIMIT = V7X_VMEM_BYTES - 8 * 1024 * 1024

ROW_TILE = 512
CONV_TILE = 256
CONV_HALO = 32
GMLP_TILE = 256
ATT_Q_TILE = 512
ATT_K_TILE = 256
COL_CHUNK = V7X_MXU_COLS


def _params(n_grid):
    return pltpu.CompilerParams(
        dimension_semantics=("arbitrary",) * n_grid, vmem_limit_bytes=VMEM_LIMIT)


def _const_spec(shape):
    return pl.BlockSpec(shape, lambda *_: (0,) * len(shape), pipeline_mode=pl.Buffered(1))


def _layer_norm(r, g, b):
    mu = jnp.mean(r, axis=-1, keepdims=True)
    c = r - mu
    var = jnp.mean(c * c, axis=-1, keepdims=True)
    return c * lax.rsqrt(var + LN_EPS) * g + b


def _ffn_kernel(x_ref, wg_ref, wv_ref, bg_ref, bv_ref, cwg_ref, cwv_ref, cbg_ref, cbv_ref,
                wd_ref, bd_ref, lg_ref, lb_ref, o_ref, xb_ref, acc_ref, h_ref, act_ref,
                carry_g_ref, carry_v_ref, *, tiles_per_seq, n_chunks, alpha):
    first = (pl.program_id(0) % tiles_per_seq) == 0
    rows, width = x_ref.shape[0], wg_ref.shape[2]
    xb_ref[...] = x_ref[...].astype(BF16)
    sub = lax.broadcasted_iota(jnp.int32, (V7X_SUBLANES, width), 0)

    def up(j, slot):
        xb = xb_ref[...]
        h_ref[slot, 0] = jnp.dot(xb, wg_ref[j], preferred_element_type=F32) + bg_ref[j]
        h_ref[slot, 1] = jnp.dot(xb, wv_ref[j], preferred_element_type=F32) + bv_ref[j]

    def conv_half(j, h, cw_ref, cb_ref, carry_ref):
        tail = jnp.where(first, 0.0, carry_ref[j])
        carry_ref[j] = h[rows - V7X_SUBLANES:]

        def shifted(k):
            r = pltpu.roll(h, k, 0)
            top = jnp.where(sub < k, pltpu.roll(tail, k, 0), r[:V7X_SUBLANES])
            return jnp.concatenate([top, r[V7X_SUBLANES:]], axis=0)

        cw = cw_ref[j]
        return cw[0:1] * shifted(2) + cw[1:2] * shifted(1) + cw[2:3] * h + cb_ref[j]

    def gate(j, slot):
        g = conv_half(j, h_ref[slot, 0], cwg_ref, cbg_ref, carry_g_ref)
        v = conv_half(j, h_ref[slot, 1], cwv_ref, cbv_ref, carry_v_ref)
        act_ref[slot] = ((g * jax.nn.sigmoid(g)) * v).astype(BF16)

    def down(j, slot, init=False):
        y = jnp.dot(act_ref[slot], wd_ref[j], preferred_element_type=F32)
        if init:
            acc_ref[...] = y
        else:
            acc_ref[...] += y

    assert n_chunks % 2 == 1 and n_chunks >= 3
    last = n_chunks - 1
    up(0, 0)
    gate(0, 0)
    up(1, 1)
    down(0, 0, init=True)
    gate(1, 1)
    up(2, 0)

    def pair_body(u, carry):
        t = 2 * u
        down(t - 1, 1)
        gate(t, 0)
        up(t + 1, 1)
        down(t, 0)
        gate(t + 1, 1)
        up(t + 2, 0)
        return carry

    lax.fori_loop(1, last // 2, pair_body, 0)
    down(last - 1, 1)
    gate(last, 0)
    down(last, 0)
    r = alpha * x_ref[...] + (acc_ref[...] + bd_ref[...])
    o_ref[...] = _layer_norm(r, lg_ref[...], lb_ref[...])


def _conv_ffn(x, seq, w_up, b_up, w_dw, b_dw, w_down, b_down, ln_g, ln_b, alpha):
    n, d = x.shape
    f = w_down.shape[0]
    tf = COL_CHUNK
    nc = f // tf
    assert nc * tf == f and n % ROW_TILE == 0 and seq % ROW_TILE == 0

    def cols(a):
        return a.reshape(a.shape[0], nc, tf).transpose(1, 0, 2)

    wg, wv = cols(w_up[:, :f].astype(BF16)), cols(w_up[:, f:].astype(BF16))
    bg, bv = cols(b_up[None, :f]), cols(b_up[None, f:])
    cwg, cwv = cols(w_dw[:, :f]), cols(w_dw[:, f:])
    cbg, cbv = cols(b_dw[None, :f]), cols(b_dw[None, f:])
    wd = w_down.astype(BF16).reshape(nc, tf, d)
    row_spec = pl.BlockSpec((ROW_TILE, d), lambda i: (i, 0))
    kern = functools.partial(_ffn_kernel, tiles_per_seq=seq // ROW_TILE, n_chunks=nc, alpha=alpha)
    return pl.pallas_call(
        kern,
        grid=(n // ROW_TILE,),
        in_specs=[row_spec] + [_const_spec(a.shape) for a in (wg, wv, bg, bv, cwg, cwv, cbg, cbv, wd)]
        + [_const_spec((1, d))] * 3,
        out_specs=row_spec,
        out_shape=jax.ShapeDtypeStruct((n, d), F32),
        scratch_shapes=[pltpu.VMEM((ROW_TILE, d), BF16),
                        pltpu.VMEM((ROW_TILE, d), F32),
                        pltpu.VMEM((2, 2, ROW_TILE, tf), F32),
                        pltpu.VMEM((2, ROW_TILE, tf), BF16),
                        pltpu.VMEM((nc, V7X_SUBLANES, tf), F32),
                        pltpu.VMEM((nc, V7X_SUBLANES, tf), F32)],
        compiler_params=_params(1),
        name="conv_ffn",
    )(x, wg, wv, bg, bv, cwg, cwv, cbg, cbv, wd, b_down[None], ln_g[None], ln_b[None])


def _proj_ln_kernel(y_ref, x_ref, w_ref, b_ref, lg_ref, lb_ref, o_ref, *, alpha):
    y = jnp.dot(y_ref[...].astype(BF16), w_ref[...], preferred_element_type=F32) + b_ref[...]
    o_ref[...] = _layer_norm(alpha * x_ref[...] + y, lg_ref[...], lb_ref[...])


def _proj_ln(y, x, w, b, ln_g, ln_b, alpha):
    n, d = x.shape
    k = y.shape[1]
    return pl.pallas_call(
        functools.partial(_proj_ln_kernel, alpha=alpha),
        grid=(n // ROW_TILE,),
        in_specs=[pl.BlockSpec((ROW_TILE, k), lambda i: (i, 0)),
                  pl.BlockSpec((ROW_TILE, d), lambda i: (i, 0)),
                  _const_spec((k, d))] + [_const_spec((1, d))] * 3,
        out_specs=pl.BlockSpec((ROW_TILE, d), lambda i: (i, 0)),
        out_shape=jax.ShapeDtypeStruct((n, d), F32),
        compiler_params=_params(1),
        name="proj_ln",
    )(y, x, w.astype(BF16), b[None], ln_g[None], ln_b[None])


def _glu_kernel(x_ref, wa_ref, wg_ref, ba_ref, bg_ref, o_ref):
    xb = x_ref[...].astype(BF16)
    for c in range(0, o_ref.shape[1], COL_CHUNK):
        sl = slice(c, c + COL_CHUNK)
        a = jnp.dot(xb, wa_ref[:, sl], preferred_element_type=F32) + ba_ref[:, sl]
        g = jnp.dot(xb, wg_ref[:, sl], preferred_element_type=F32) + bg_ref[:, sl]
        o_ref[:, sl] = a * jax.nn.sigmoid(g)


def _glu(x, w_in, b_in):
    n, d = x.shape
    w = w_in.astype(BF16)
    row_spec = pl.BlockSpec((ROW_TILE, d), lambda i: (i, 0))
    return pl.pallas_call(
        _glu_kernel,
        grid=(n // ROW_TILE,),
        in_specs=[row_spec, _const_spec((d, d)), _const_spec((d, d)),
                  _const_spec((1, d)), _const_spec((1, d))],
        out_specs=row_spec,
        out_shape=jax.ShapeDtypeStruct((n, d), F32),
        compiler_params=_params(1),
        name="glu_in",
    )(x, w[:, :d], w[:, d:], b_in[None, :d], b_in[None, d:])


def _conv_out_kernel(h_ref, halo_ref, x_ref, cw_ref, cb_ref, ng_ref, nb_ref, w_ref, b_ref,
                     lg_ref, lb_ref, o_ref, buf_ref, conv_ref, *, tiles_per_seq, alpha):
    first = (pl.program_id(0) % tiles_per_seq) == 0
    rows = h_ref.shape[0]
    buf_ref[:CONV_HALO] = jnp.where(first, 0.0, halo_ref[...])
    buf_ref[CONV_HALO:] = h_ref[...]
    lead = CONV_HALO - (CONV_KERNEL - 1)
    for c in range(0, h_ref.shape[1], COL_CHUNK):
        sl = slice(c, c + COL_CHUNK)
        buf = buf_ref[:, sl]
        cw = cw_ref[:, sl]
        acc = jnp.zeros((rows, COL_CHUNK), F32) + cb_ref[:, sl]
        for r in range(V7X_SUBLANES):
            shifted = buf if r == 0 else pltpu.roll(buf, buf.shape[0] - r, 0)
            for k in range(CONV_KERNEL):
                off = lead + k
                if off % V7X_SUBLANES == r:
                    base = off - r
                    acc = acc + cw[k:k + 1] * shifted[base:base + rows]
        conv_ref[:, sl] = acc
    hn = _layer_norm(conv_ref[...], ng_ref[...], nb_ref[...])
    hn = hn * jax.nn.sigmoid(hn)
    y = jnp.dot(hn.astype(BF16), w_ref[...], preferred_element_type=F32) + b_ref[...]
    o_ref[...] = _layer_norm(alpha * x_ref[...] + y, lg_ref[...], lb_ref[...])


def _conv_out(h, x, seq, w_dw, b_dw, n_g, n_b, w_out, b_out, ln_g, ln_b, alpha):
    n, d = x.shape
    t = CONV_TILE
    per_halo = t // CONV_HALO
    row_spec = pl.BlockSpec((t, d), lambda i: (i, 0))
    halo_spec = pl.BlockSpec((CONV_HALO, d), lambda i: (jnp.maximum(i * per_halo - 1, 0), 0))
    kern = functools.partial(_conv_out_kernel, tiles_per_seq=seq // t, alpha=alpha)
    return pl.pallas_call(
        kern,
        grid=(n // t,),
        in_specs=[row_spec, halo_spec, row_spec, _const_spec((CONV_KERNEL, d))]
        + [_const_spec((1, d))] * 3 + [_const_spec((d, d))] + [_const_spec((1, d))] * 3,
        out_specs=row_spec,
        out_shape=jax.ShapeDtypeStruct((n, d), F32),
        scratch_shapes=[pltpu.VMEM((CONV_HALO + t, d), F32), pltpu.VMEM((t, d), F32)],
        compiler_params=_params(1),
        name="conv_out",
    )(h, h, x, w_dw, b_dw[None], n_g[None], n_b[None], w_out.astype(BF16), b_out[None],
      ln_g[None], ln_b[None])


def _qkv_kernel(x_ref, w_ref, q_ref, k_ref, v_ref, *, scale):
    xb = x_ref[...].astype(BF16)
    d = q_ref.shape[1]
    for idx, (ref, mul) in enumerate(((q_ref, scale), (k_ref, None), (v_ref, None))):
        for c in range(0, d, COL_CHUNK):
            y = jnp.dot(xb, w_ref[:, idx * d + c: idx * d + c + COL_CHUNK],
                        preferred_element_type=F32)
            if mul is not None:
                y = y * mul
            ref[:, c:c + COL_CHUNK] = y.astype(BF16)


def _qkv(x, w_qkv):
    n, d = x.shape
    row_spec = pl.BlockSpec((ROW_TILE, d), lambda i: (i, 0))
    out = jax.ShapeDtypeStruct((n, d), BF16)
    return pl.pallas_call(
        functools.partial(_qkv_kernel, scale=DA_HEAD_DIM ** -0.5),
        grid=(n // ROW_TILE,),
        in_specs=[row_spec, _const_spec((d, 3 * d))],
        out_specs=[row_spec] * 3,
        out_shape=[out] * 3,
        compiler_params=_params(1),
        name="qkv",
    )(x, w_qkv.astype(BF16))


def _attn_kernel(lam_ref, qt_ref, k_ref, vt_ref, g_ref, o_ref,
                 m_ref, l_ref, acc_ref, s_ref, p_ref, a_ref, *, out_scale):
    qi = pl.program_id(2)
    tq = qt_ref.shape[-1]
    tk = vt_ref.shape[-1]
    per_q = tq // tk
    qt = qt_ref[0, 0, 0]
    comp = lax.broadcasted_iota(jnp.int32, qt.shape, 0) < DA_HEAD_DIM
    qts = (jnp.where(comp, qt, jnp.zeros_like(qt)), jnp.where(comp, jnp.zeros_like(qt), qt))

    def scores(j, slot):
        kb = k_ref[0, pl.ds(pl.multiple_of(j * tk, tk), tk), :]
        for c in range(2):
            s_ref[slot, c] = jnp.dot(kb, qts[c], preferred_element_type=F32)

    def softmax(slot, diag):
        for c in range(2):
            s = s_ref[slot, c]
            if diag is not None:
                key = lax.broadcasted_iota(jnp.int32, s.shape, 0) + diag * tk
                qry = lax.broadcasted_iota(jnp.int32, s.shape, 1)
                s = jnp.where(key <= qry, s, -jnp.inf)
            m_old = m_ref[c]
            m_new = jnp.maximum(m_old, jnp.max(s, axis=0, keepdims=True))
            a = jnp.exp(m_old - m_new)
            p = jnp.exp(s - m_new)
            l_ref[c] = a * l_ref[c] + jnp.sum(p, axis=0, keepdims=True)
            m_ref[c] = m_new
            a_ref[slot, c] = a
            p_ref[slot, c] = p.astype(BF16)

    def values(j, slot):
        vt = vt_ref[0, 0, j]
        for c in range(2):
            acc_ref[c] = a_ref[slot, c] * acc_ref[c] + jnp.dot(
                vt, p_ref[slot, c], preferred_element_type=F32)

    m_ref[...] = jnp.full_like(m_ref, -jnp.inf)
    l_ref[...] = jnp.zeros_like(l_ref)
    acc_ref[...] = jnp.zeros_like(acc_ref)
    p_ref[1] = jnp.zeros_like(p_ref[1])
    a_ref[1] = jnp.ones_like(a_ref[1])

    first_diag = qi * per_q
    scores(0, 0)

    def pair(u, carry):
        j = 2 * u
        values(jnp.maximum(j - 1, 0), 1)
        softmax(0, None)
        scores(j + 1, 1)
        values(j, 0)
        softmax(1, None)
        scores(j + 2, 0)
        return carry

    assert per_q == 2
    lax.fori_loop(0, qi, pair, 0)
    values(jnp.maximum(first_diag - 1, 0), 1)
    softmax(0, 0)
    scores(first_diag + 1, 1)
    values(first_diag, 0)
    softmax(1, 1)
    values(first_diag + 1, 1)

    lam = lam_ref[0, 0]
    o = acc_ref[0] / l_ref[0] - lam * (acc_ref[1] / l_ref[1])
    o = o * lax.rsqrt(jnp.mean(o * o, axis=0, keepdims=True) + RMS_EPS) * g_ref[...] * out_scale
    o_ref[...] = o.T.astype(o_ref.dtype)


def _attention(q, k, v, lam, subln_g, bsz, seq, out_scale):
    n, d = q.shape
    tq, tk = ATT_Q_TILE, ATT_K_TILE
    nq, nk = seq // tq, seq // tk

    def transposed(a, t):
        return a.reshape(bsz, seq // t, t, DA_HEADS, HEAD_W).transpose(0, 3, 1, 4, 2)

    qt, vt = transposed(q, tq), transposed(v, tk)
    k3 = k.reshape(bsz, seq, d)
    return pl.pallas_call(
        functools.partial(_attn_kernel, out_scale=out_scale),
        grid=(bsz, DA_HEADS, nq),
        in_specs=[pl.BlockSpec(memory_space=pltpu.SMEM),
                  pl.BlockSpec((1, 1, 1, HEAD_W, tq), lambda b, h, i: (b, h, i, 0, 0)),
                  pl.BlockSpec((1, seq, HEAD_W), lambda b, h, i: (b, 0, h)),
                  pl.BlockSpec((1, 1, nk, HEAD_W, tk), lambda b, h, i: (b, h, 0, 0, 0)),
                  pl.BlockSpec((HEAD_W, 1), lambda b, h, i: (0, 0))],
        out_specs=pl.BlockSpec((tq, HEAD_W), lambda b, h, i: (b * nq + i, h)),
        out_shape=jax.ShapeDtypeStruct((n, d), BF16),
        scratch_shapes=[pltpu.VMEM((2, 1, tq), F32), pltpu.VMEM((2, 1, tq), F32),
                        pltpu.VMEM((2, HEAD_W, tq), F32),
                        pltpu.VMEM((2, 2, tk, tq), F32), pltpu.VMEM((2, 2, tk, tq), BF16),
                        pltpu.VMEM((2, 2, 1, tq), F32)],
        compiler_params=_params(3),
        name="diff_attention",
    )(lam.reshape(1, 1), qt, k3, vt, subln_g[:, None])


def _gelu_in_kernel(x_ref, w_ref, b_ref, o_ref):
    xb = x_ref[...].astype(BF16)
    for c in range(0, o_ref.shape[1], COL_CHUNK):
        sl = slice(c, c + COL_CHUNK)
        z = jnp.dot(xb, w_ref[:, sl], preferred_element_type=F32) + b_ref[:, sl]
        o_ref[:, sl] = 0.5 * z * (1.0 + lax.erf(z * math.sqrt(0.5)))


def _gelu_in(x, w_in, b_in, tn):
    n, d = x.shape
    f = w_in.shape[1]
    return pl.pallas_call(
        _gelu_in_kernel,
        grid=(n // ROW_TILE, f // tn),
        in_specs=[pl.BlockSpec((ROW_TILE, d), lambda i, j: (i, 0)),
                  pl.BlockSpec((d, tn), lambda i, j: (0, j)),
                  pl.BlockSpec((1, tn), lambda i, j: (0, j))],
        out_specs=pl.BlockSpec((ROW_TILE, tn), lambda i, j: (i, j)),
        out_shape=jax.ShapeDtypeStruct((n, f), F32),
        compiler_params=_params(2),
        name="gelu_in",
    )(x, w_in.astype(BF16), b_in[None])


def _gate_out_kernel(u_ref, v_ref, x_ref, ng_ref, nb_ref, ws_ref, bs_ref, w_ref, b_ref,
                     lg_ref, lb_ref, o_ref, gated_ref, *, alpha):
    rows, half = u_ref.shape
    gw = half // GMLP_GROUPS
    vn = _layer_norm(v_ref[...], ng_ref[...], nb_ref[...]).astype(BF16)
    tri_r = lax.broadcasted_iota(jnp.int32, (CHUNK, CHUNK), 0)
    tri_c = lax.broadcasted_iota(jnp.int32, (CHUNK, CHUNK), 1)
    for g in range(GMLP_GROUPS):
        ws = jnp.where(tri_c <= tri_r, ws_ref[g], 0.0).astype(BF16)
        for r in range(0, rows, CHUNK):
            cs = slice(g * gw, (g + 1) * gw)
            sv = jnp.dot(ws, vn[r:r + CHUNK, cs], preferred_element_type=F32) + bs_ref[g]
            gated_ref[r:r + CHUNK, cs] = (u_ref[r:r + CHUNK, cs] * sv).astype(BF16)
    y = jnp.dot(gated_ref[...], w_ref[...], preferred_element_type=F32) + b_ref[...]
    o_ref[...] = _layer_norm(alpha * x_ref[...] + y, lg_ref[...], lb_ref[...])


def _gate_out(z, x, n_g, n_b, w_s, b_s, w_out, b_out, ln_g, ln_b, alpha):
    n, d = x.shape
    half = z.shape[1] // 2
    t = GMLP_TILE
    return pl.pallas_call(
        functools.partial(_gate_out_kernel, alpha=alpha),
        grid=(n // t,),
        in_specs=[pl.BlockSpec((t, half), lambda i: (i, 0)),
                  pl.BlockSpec((t, half), lambda i: (i, 1)),
                  pl.BlockSpec((t, d), lambda i: (i, 0)),
                  _const_spec((1, half)), _const_spec((1, half)),
                  _const_spec((GMLP_GROUPS, CHUNK, CHUNK)), _const_spec((GMLP_GROUPS, CHUNK, 1)),
                  _const_spec((half, d))] + [_const_spec((1, d))] * 3,
        out_specs=pl.BlockSpec((t, d), lambda i: (i, 0)),
        out_shape=jax.ShapeDtypeStruct((n, d), F32),
        scratch_shapes=[pltpu.VMEM((t, half), BF16)],
        compiler_params=_params(1),
        name="gate_out",
    )(z, z, x, n_g[None], n_b[None], w_s, b_s[:, :, None], w_out.astype(BF16), b_out[None],
      ln_g[None], ln_b[None])


def kernel(x, a_w_in, a_b_in, a_w_dw, a_b_dw, a_ln_g, a_ln_b, a_w_out, a_b_out, b_w_qkv, b_lq1, b_lk1, b_lq2, b_lk2, b_subln_g, b_w_o, c_w_in, c_b_in, c_ln_g, c_ln_b, c_w_s, c_b_s, c_w_out, c_b_out, f_w_up, f_b_up, f_w_dw, f_b_dw, f_w_down, f_b_down, ln_mix_g, ln_mix_b, ln_ffn_g, ln_ffn_b):
    bsz, seq, d = x.shape
    depth = f_w_up.shape[0]
    alpha = (2 * depth) ** 0.25
    x = x.reshape(bsz * seq, d)
    for i in range(depth):
        kind, j = i % N_MIXERS, i // N_MIXERS
        mix_ln = (ln_mix_g[i], ln_mix_b[i], alpha)
        if kind == 0:
            h = _glu(x, a_w_in[j], a_b_in[j])
            x = _conv_out(h, x, seq, a_w_dw[j], a_b_dw[j], a_ln_g[j], a_ln_b[j],
                          a_w_out[j], a_b_out[j], *mix_ln)
        elif kind == 1:
            lambda_init = 0.8 - 0.6 * math.exp(-0.3 * i)
            lam = (jnp.exp(jnp.sum(b_lq1[j] * b_lk1[j])) - jnp.exp(jnp.sum(b_lq2[j] * b_lk2[j]))
                   + lambda_init)
            q, k, v = _qkv(x, b_w_qkv[j])
            o = _attention(q, k, v, lam, b_subln_g[j], bsz, seq, 1.0 - lambda_init)
            x = _proj_ln(o, x, b_w_o[j], jnp.zeros((d,), F32), *mix_ln)
        else:
            z = _gelu_in(x, c_w_in[j], c_b_in[j], 1024)
            x = _gate_out(z, x, c_ln_g[j], c_ln_b[j], c_w_s[j], c_b_s[j],
                          c_w_out[j], c_b_out[j], *mix_ln)
        x = _conv_ffn(x, seq, f_w_up[i], f_b_up[i], f_w_dw[i], f_b_dw[i],
                      f_w_down[i], f_b_down[i], ln_ffn_g[i], ln_ffn_b[i], alpha)
    return x.reshape(bsz, seq, d)
```

```python
import functools
import math

import jax
import jax.numpy as jnp
from jax import lax
from jax.experimental import pallas as pl
from jax.experimental.pallas import tpu as pltpu

F32 = jnp.float32
BF16 = jnp.bfloat16

N_MIXERS = 3
CONV_KERNEL = 31
DA_HEADS = 8
DA_HEAD_DIM = 64
HEAD_W = 2 * DA_HEAD_DIM
GMLP_GROUPS = 4
CHUNK = 128
FFN_CONV = 3
LN_EPS = 1e-5
RMS_EPS = 1e-5
LOG2_E = math.log2(math.e)

V7X_SUBLANES = 8
V7X_BF16_SUBLANES = 16
V7X_MXU_COLS = 256
V7X_VMEM_BYTES = 64 * 1024 * 1024
VMEM_LIMIT = V7X_VMEM_BYTES - 8 * 1024 * 1024

ROW_TILE = 512
CONV_HALO_BLOCKS = 32
CONV_STRIP = 64
ATT_Q_TILE = 512
ATT_K_TILE = 256
V_PAD = V7X_BF16_SUBLANES
COL_CHUNK = V7X_MXU_COLS


def _params(n_grid):
    return pltpu.CompilerParams(
        dimension_semantics=("arbitrary",) * n_grid, vmem_limit_bytes=VMEM_LIMIT)


def _const_spec(shape):
    return pl.BlockSpec(shape, lambda *_: (0,) * len(shape), pipeline_mode=pl.Buffered(1))


def _tile_order_positions(t):
    i = jnp.arange(t)
    return (i % V7X_SUBLANES) * (t // V7X_SUBLANES) + i // V7X_SUBLANES


def _to_tile_order(x, t):
    n, d = x.shape
    return x.reshape(n // t, V7X_SUBLANES, t // V7X_SUBLANES, d).transpose(0, 2, 1, 3).reshape(n, d)


def _from_tile_order(x, t):
    n, d = x.shape
    return x.reshape(n // t, t // V7X_SUBLANES, V7X_SUBLANES, d).transpose(0, 2, 1, 3).reshape(n, d)


def _segment_shift(cur, prev):
    n = cur.shape[0]
    sub = lax.broadcasted_iota(jnp.int32, cur.shape, 0) % V7X_SUBLANES
    return jnp.where(sub == 0, pltpu.roll(prev, n - (V7X_SUBLANES - 1), 0), pltpu.roll(cur, 1, 0))


def _layer_norm(r, g, b):
    mu = jnp.mean(r, axis=-1, keepdims=True)
    c = r - mu
    var = jnp.mean(c * c, axis=-1, keepdims=True)
    return c * lax.rsqrt(var + LN_EPS) * g + b


def _ffn_kernel(x_ref, wg_ref, wv_ref, bg_ref, bv_ref, cwg_ref, cwv_ref, cbg_ref, cbv_ref,
                wd_ref, bd_ref, lg_ref, lb_ref, o_ref, xb_ref, acc_ref, h_ref, act_ref,
                carry_ref, *, tiles_per_seq, n_chunks, alpha):
    first = (pl.program_id(0) % tiles_per_seq) == 0
    parity = pl.program_id(0) % 2
    rows = x_ref.shape[0]
    hist = (FFN_CONV - 1) * V7X_SUBLANES
    xb_ref[...] = x_ref[...].astype(BF16)

    def up(j, slot):
        xb = xb_ref[...]
        for half, (w_ref, b_ref) in enumerate(((wg_ref, bg_ref), (wv_ref, bv_ref))):
            h = jnp.dot(xb, w_ref[j], preferred_element_type=F32) + b_ref[j]
            h_ref[slot, half] = h
            carry_ref[parity, half, j] = h[rows - hist:]

    def gate(j, slot):
        conv = []
        for half, (cw_ref, cb_ref) in enumerate(((cwg_ref, cbg_ref), (cwv_ref, cbv_ref))):
            cw = cw_ref[j]
            h = h_ref[slot, half]
            prev = jnp.where(first, 0.0, carry_ref[1 - parity, half, j])
            hs = _segment_shift(h[rows - hist:], prev)
            back1 = jnp.concatenate([hs[V7X_SUBLANES:], h[:rows - V7X_SUBLANES]], axis=0)
            back2 = jnp.concatenate([hs, h[:rows - hist]], axis=0)
            conv.append(cw[0:1] * back2 + cw[1:2] * back1 + cw[2:3] * h + cb_ref[j])
        g, v = conv
        act_ref[slot] = ((g * jax.nn.sigmoid(g)) * v).astype(BF16)

    def down(j, slot):
        acc_ref[...] += jnp.dot(act_ref[slot], wd_ref[j], preferred_element_type=F32)

    def step(t, slot, has_down=True, has_up=True):
        if has_up:
            up(t + 1, 1 - slot)
        gate(t, slot)
        if has_down:
            down(t - 1, 1 - slot)

    assert n_chunks % 2 == 1 and n_chunks >= 3
    last = n_chunks - 1
    acc_ref[...] = jnp.zeros_like(acc_ref)
    up(0, 0)
    step(0, 0, has_down=False)
    step(1, 1)

    def pair_body(u, carry):
        step(2 * u, 0)
        step(2 * u + 1, 1)
        return carry

    lax.fori_loop(1, last // 2, pair_body, 0)
    step(last, 0, has_up=False)
    down(last, 0)
    r = alpha * x_ref[...] + (acc_ref[...] + bd_ref[...])
    o_ref[...] = _layer_norm(r, lg_ref[...], lb_ref[...])


def _conv_ffn(x, seq, w_up, b_up, w_dw, b_dw, w_down, b_down, ln_g, ln_b, alpha):
    n, d = x.shape
    f = w_down.shape[0]
    tf = COL_CHUNK
    nc = f // tf
    assert nc * tf == f and n % ROW_TILE == 0 and seq % ROW_TILE == 0

    def cols(a):
        return a.reshape(a.shape[0], nc, tf).transpose(1, 0, 2)

    wg, wv = cols(w_up[:, :f].astype(BF16)), cols(w_up[:, f:].astype(BF16))
    bg, bv = cols(b_up[None, :f]), cols(b_up[None, f:])
    cwg, cwv = cols(w_dw[:, :f]), cols(w_dw[:, f:])
    cbg, cbv = cols(b_dw[None, :f]), cols(b_dw[None, f:])
    wd = w_down.astype(BF16).reshape(nc, tf, d)
    row_spec = pl.BlockSpec((ROW_TILE, d), lambda i: (i, 0))
    kern = functools.partial(_ffn_kernel, tiles_per_seq=seq // ROW_TILE, n_chunks=nc, alpha=alpha)
    return pl.pallas_call(
        kern,
        grid=(n // ROW_TILE,),
        in_specs=[row_spec] + [_const_spec(a.shape) for a in (wg, wv, bg, bv, cwg, cwv, cbg, cbv, wd)]
        + [_const_spec((1, d))] * 3,
        out_specs=row_spec,
        out_shape=jax.ShapeDtypeStruct((n, d), F32),
        scratch_shapes=[pltpu.VMEM((ROW_TILE, d), BF16),
                        pltpu.VMEM((ROW_TILE, d), F32),
                        pltpu.VMEM((2, 2, ROW_TILE, tf), F32),
                        pltpu.VMEM((2, ROW_TILE, tf), BF16),
                        pltpu.VMEM((2, 2, nc, (FFN_CONV - 1) * V7X_SUBLANES, tf), F32)],
        compiler_params=_params(1),
        name="conv_ffn",
    )(x, wg, wv, bg, bv, cwg, cwv, cbg, cbv, wd, b_down[None], ln_g[None], ln_b[None])


def _proj_ln_kernel(y_ref, x_ref, w_ref, b_ref, lg_ref, lb_ref, o_ref, *, alpha):
    y = jnp.dot(y_ref[...].astype(BF16), w_ref[...], preferred_element_type=F32) + b_ref[...]
    o_ref[...] = _layer_norm(alpha * x_ref[...] + y, lg_ref[...], lb_ref[...])


def _proj_ln(y, x, w, b, ln_g, ln_b, alpha):
    n, d = x.shape
    k = y.shape[1]
    return pl.pallas_call(
        functools.partial(_proj_ln_kernel, alpha=alpha),
        grid=(n // ROW_TILE,),
        in_specs=[pl.BlockSpec((ROW_TILE, k), lambda i: (i, 0)),
                  pl.BlockSpec((ROW_TILE, d), lambda i: (i, 0)),
                  _const_spec((k, d))] + [_const_spec((1, d))] * 3,
        out_specs=pl.BlockSpec((ROW_TILE, d), lambda i: (i, 0)),
        out_shape=jax.ShapeDtypeStruct((n, d), F32),
        compiler_params=_params(1),
        name="proj_ln",
    )(y, x, w.astype(BF16), b[None], ln_g[None], ln_b[None])


def _glu_kernel(x_ref, wa_ref, wg_ref, ba_ref, bg_ref, o_ref):
    xb = x_ref[...].astype(BF16)
    for c in range(0, o_ref.shape[1], COL_CHUNK):
        sl = slice(c, c + COL_CHUNK)
        a = jnp.dot(xb, wa_ref[:, sl], preferred_element_type=F32) + ba_ref[:, sl]
        g = jnp.dot(xb, wg_ref[:, sl], preferred_element_type=F32) + bg_ref[:, sl]
        o_ref[:, sl] = a * jax.nn.sigmoid(g)


def _glu(x, w_in, b_in):
    n, d = x.shape
    w = w_in.astype(BF16)
    row_spec = pl.BlockSpec((ROW_TILE, d), lambda i: (i, 0))
    return pl.pallas_call(
        _glu_kernel,
        grid=(n // ROW_TILE,),
        in_specs=[row_spec, _const_spec((d, d)), _const_spec((d, d)),
                  _const_spec((1, d)), _const_spec((1, d))],
        out_specs=row_spec,
        out_shape=jax.ShapeDtypeStruct((n, d), F32),
        compiler_params=_params(1),
        name="glu_in",
    )(x, w[:, :d], w[:, d:], b_in[None, :d], b_in[None, d:])


def _conv_out_kernel(h_ref, halo_ref, x_ref, cw_ref, cb_ref, ng_ref, nb_ref, w_ref, b_ref,
                     lg_ref, lb_ref, o_ref, buf_ref, conv_ref, *, tiles_per_seq, alpha):
    first = (pl.program_id(0) % tiles_per_seq) == 0
    rows = h_ref.shape[0]
    hist = CONV_HALO_BLOCKS * V7X_SUBLANES
    prev = jnp.where(first, 0.0, halo_ref[...])
    buf_ref[:hist] = _segment_shift(h_ref[rows - hist:], prev)
    buf_ref[hist:] = h_ref[...]
    lead = CONV_HALO_BLOCKS - (CONV_KERNEL - 1)
    for c in range(0, h_ref.shape[1], COL_CHUNK):
        sl = slice(c, c + COL_CHUNK)
        cw = cw_ref[:, sl]

        def strip(i, carry):
            r0 = pl.multiple_of(i * CONV_STRIP, CONV_STRIP)
            acc = jnp.zeros((CONV_STRIP, COL_CHUNK), F32) + cb_ref[:, sl]
            for k in range(CONV_KERNEL):
                acc = acc + cw[k:k + 1] * buf_ref[pl.ds(r0 + (lead + k) * V7X_SUBLANES, CONV_STRIP), sl]
            conv_ref[pl.ds(r0, CONV_STRIP), sl] = acc
            return carry

        lax.fori_loop(0, rows // CONV_STRIP, strip, 0)
    hn = _layer_norm(conv_ref[...], ng_ref[...], nb_ref[...])
    hn = hn * jax.nn.sigmoid(hn)
    y = jnp.dot(hn.astype(BF16), w_ref[...], preferred_element_type=F32) + b_ref[...]
    o_ref[...] = _layer_norm(alpha * x_ref[...] + y, lg_ref[...], lb_ref[...])


def _conv_out(h, x, seq, w_dw, b_dw, n_g, n_b, w_out, b_out, ln_g, ln_b, alpha):
    n, d = x.shape
    t = ROW_TILE
    hist = CONV_HALO_BLOCKS * V7X_SUBLANES
    per_halo = t // hist
    row_spec = pl.BlockSpec((t, d), lambda i: (i, 0))
    halo_spec = pl.BlockSpec((hist, d), lambda i: (jnp.maximum(i * per_halo - 1, 0), 0))
    kern = functools.partial(_conv_out_kernel, tiles_per_seq=seq // t, alpha=alpha)
    return pl.pallas_call(
        kern,
        grid=(n // t,),
        in_specs=[row_spec, halo_spec, row_spec, _const_spec((CONV_KERNEL, d))]
        + [_const_spec((1, d))] * 3 + [_const_spec((d, d))] + [_const_spec((1, d))] * 3,
        out_specs=row_spec,
        out_shape=jax.ShapeDtypeStruct((n, d), F32),
        scratch_shapes=[pltpu.VMEM((hist + t, d), F32), pltpu.VMEM((t, d), F32)],
        compiler_params=_params(1),
        name="conv_out",
    )(h, h, x, w_dw, b_dw[None], n_g[None], n_b[None], w_out.astype(BF16), b_out[None],
      ln_g[None], ln_b[None])


def _qkv_kernel(x_ref, w_ref, q_ref, k_ref, v_ref, *, scale):
    xb = x_ref[...].astype(BF16)
    d = q_ref.shape[1]
    for idx, (ref, mul) in enumerate(((q_ref, scale), (k_ref, None), (v_ref, None))):
        for c in range(0, d, COL_CHUNK):
            y = jnp.dot(xb, w_ref[:, idx * d + c: idx * d + c + COL_CHUNK],
                        preferred_element_type=F32)
            if mul is not None:
                y = y * mul
            ref[:, c:c + COL_CHUNK] = y.astype(BF16)


def _qkv(x, w_qkv):
    n, d = x.shape
    row_spec = pl.BlockSpec((ROW_TILE, d), lambda i: (i, 0))
    out = jax.ShapeDtypeStruct((n, d), BF16)
    return pl.pallas_call(
        functools.partial(_qkv_kernel, scale=DA_HEAD_DIM ** -0.5 * LOG2_E),
        grid=(n // ROW_TILE,),
        in_specs=[row_spec, _const_spec((d, 3 * d))],
        out_specs=[row_spec] * 3,
        out_shape=[out] * 3,
        compiler_params=_params(1),
        name="qkv",
    )(x, w_qkv.astype(BF16))


def _attn_kernel(lam_ref, qt_ref, k_ref, vt_ref, g_ref, o_ref,
                 m_ref, acc_ref, s_ref, p_ref, a_ref, *, out_scale):
    qi = pl.program_id(2)
    tq = qt_ref.shape[-1]
    tk = vt_ref.shape[-1]
    per_q = tq // tk
    qt = qt_ref[0, 0, 0]
    comp = lax.broadcasted_iota(jnp.int32, qt.shape, 0) < DA_HEAD_DIM
    qts = (jnp.where(comp, qt, jnp.zeros_like(qt)), jnp.where(comp, jnp.zeros_like(qt), qt))

    def position(i):
        return (i % V7X_SUBLANES) * (tq // V7X_SUBLANES) + i // V7X_SUBLANES

    def scores(j, slot):
        kb = k_ref[0, pl.ds(pl.multiple_of(j * tk, tk), tk), :]
        for c in range(2):
            s_ref[slot, c] = jnp.dot(kb, qts[c], preferred_element_type=F32)

    def softmax(slot, diag):
        for c in range(2):
            s = s_ref[slot, c]
            if diag is not None:
                key = position(lax.broadcasted_iota(jnp.int32, s.shape, 0) + diag * tk)
                qry = position(lax.broadcasted_iota(jnp.int32, s.shape, 1))
                s = jnp.where(key <= qry, s, -jnp.inf)
            m_old = m_ref[c]
            m_new = jnp.maximum(m_old, jnp.max(s, axis=0, keepdims=True))
            m_ref[c] = m_new
            a_ref[slot, c] = jnp.exp2(m_old - m_new)
            p_ref[slot, c] = jnp.exp2(s - m_new).astype(BF16)

    def values(j, slot):
        vt = vt_ref[0, 0, j]
        for c in range(2):
            acc_ref[c] = a_ref[slot, c] * acc_ref[c] + jnp.dot(
                vt, p_ref[slot, c], preferred_element_type=F32)

    m_ref[...] = jnp.full_like(m_ref, -jnp.inf)
    acc_ref[...] = jnp.zeros_like(acc_ref)
    p_ref[1] = jnp.zeros_like(p_ref[1])
    a_ref[1] = jnp.ones_like(a_ref[1])

    first_diag = qi * per_q
    scores(0, 0)

    def pair(u, carry):
        j = 2 * u
        values(jnp.maximum(j - 1, 0), 1)
        softmax(0, None)
        scores(j + 1, 1)
        values(j, 0)
        softmax(1, None)
        scores(j + 2, 0)
        return carry

    assert per_q == 2 and tq == ROW_TILE
    lax.fori_loop(0, qi, pair, 0)
    values(jnp.maximum(first_diag - 1, 0), 1)
    softmax(0, 0)
    scores(first_diag + 1, 1)
    values(first_diag, 0)
    softmax(1, 1)
    values(first_diag + 1, 1)

    lam = lam_ref[0, 0]
    o = (acc_ref[0, :HEAD_W] / acc_ref[0, HEAD_W:HEAD_W + 1]
         - lam * (acc_ref[1, :HEAD_W] / acc_ref[1, HEAD_W:HEAD_W + 1]))
    o = o * lax.rsqrt(jnp.mean(o * o, axis=0, keepdims=True) + RMS_EPS) * g_ref[...] * out_scale
    o_ref[...] = o.T.astype(o_ref.dtype)


def _attention(q, k, v, lam, subln_g, bsz, seq, out_scale):
    n, d = q.shape
    tq, tk = ATT_Q_TILE, ATT_K_TILE
    nq, nk = seq // tq, seq // tk

    def transposed(a, t):
        return a.reshape(bsz, seq // t, t, DA_HEADS, HEAD_W).transpose(0, 3, 1, 4, 2)

    qt, vt = transposed(q, tq), transposed(v, tk)
    ones = jnp.zeros(vt.shape[:3] + (V_PAD, tk), BF16).at[..., 0, :].set(1)
    vt = jnp.concatenate([vt, ones], axis=3)
    k3 = k.reshape(bsz, seq, d)
    return pl.pallas_call(
        functools.partial(_attn_kernel, out_scale=out_scale),
        grid=(bsz, DA_HEADS, nq),
        in_specs=[pl.BlockSpec(memory_space=pltpu.SMEM),
                  pl.BlockSpec((1, 1, 1, HEAD_W, tq), lambda b, h, i: (b, h, i, 0, 0)),
                  pl.BlockSpec((1, seq, HEAD_W), lambda b, h, i: (b, 0, h)),
                  pl.BlockSpec((1, 1, nk, HEAD_W + V_PAD, tk), lambda b, h, i: (b, h, 0, 0, 0)),
                  pl.BlockSpec((HEAD_W, 1), lambda b, h, i: (0, 0))],
        out_specs=pl.BlockSpec((tq, HEAD_W), lambda b, h, i: (b * nq + i, h)),
        out_shape=jax.ShapeDtypeStruct((n, d), BF16),
        scratch_shapes=[pltpu.VMEM((2, 1, tq), F32),
                        pltpu.VMEM((2, HEAD_W + V_PAD, tq), F32),
                        pltpu.VMEM((2, 2, tk, tq), F32), pltpu.VMEM((2, 2, tk, tq), BF16),
                        pltpu.VMEM((2, 2, 1, tq), F32)],
        compiler_params=_params(3),
        name="diff_attention",
    )(lam.reshape(1, 1), qt, k3, vt, subln_g[:, None])


def _gmlp_kernel(x_ref, wi_ref, bi_ref, ng_ref, nb_ref, ws_ref, bs_ref, wo_ref, bo_ref,
                 lg_ref, lb_ref, o_ref, u_ref, v_ref, vn_ref, *, alpha):
    rows, half = u_ref.shape
    gw = half // GMLP_GROUPS
    xb = x_ref[...].astype(BF16)
    for c in range(0, 2 * half, COL_CHUNK):
        z = jnp.dot(xb, wi_ref[:, c:c + COL_CHUNK], preferred_element_type=F32) + bi_ref[:, c:c + COL_CHUNK]
        z = 0.5 * z * (1.0 + lax.erf(z * math.sqrt(0.5)))
        if c < half:
            u_ref[:, c:c + COL_CHUNK] = z
        else:
            v_ref[:, c - half:c - half + COL_CHUNK] = z
    for r0 in range(0, rows, CHUNK):
        vn_ref[r0:r0 + CHUNK] = _layer_norm(v_ref[r0:r0 + CHUNK], ng_ref[...], nb_ref[...]).astype(BF16)
    for g in range(GMLP_GROUPS):
        cs = slice(g * gw, (g + 1) * gw)
        sv = jnp.dot(ws_ref[g], vn_ref[:, cs], preferred_element_type=F32) + bs_ref[g]
        vn_ref[:, cs] = (u_ref[:, cs] * sv).astype(BF16)
    y = jnp.dot(vn_ref[...], wo_ref[...], preferred_element_type=F32) + bo_ref[...]
    o_ref[...] = _layer_norm(alpha * x_ref[...] + y, lg_ref[...], lb_ref[...])


def _spatial_weights(w_s, b_s, t):
    pos = _tile_order_positions(t)
    chunk, p = pos // CHUNK, pos % CHUNK
    keep = (chunk[:, None] == chunk[None, :]) & (p[None, :] <= p[:, None])
    w = jnp.where(keep[None], w_s[:, p[:, None], p[None, :]], 0.0)
    return w.astype(BF16), b_s[:, p][:, :, None]


def _gmlp(x, w_in, b_in, n_g, n_b, w_s, b_s, w_out, b_out, ln_g, ln_b, alpha):
    n, d = x.shape
    f = w_in.shape[1]
    half = f // 2
    t = ROW_TILE
    ws, bs = _spatial_weights(w_s, b_s, t)
    row_spec = pl.BlockSpec((t, d), lambda i: (i, 0))
    return pl.pallas_call(
        functools.partial(_gmlp_kernel, alpha=alpha),
        grid=(n // t,),
        in_specs=[row_spec, _const_spec((d, f)), _const_spec((1, f)),
                  _const_spec((1, half)), _const_spec((1, half)),
                  _const_spec((GMLP_GROUPS, t, t)), _const_spec((GMLP_GROUPS, t, 1)),
                  _const_spec((half, d))] + [_const_spec((1, d))] * 3,
        out_specs=row_spec,
        out_shape=jax.ShapeDtypeStruct((n, d), F32),
        scratch_shapes=[pltpu.VMEM((t, half), F32), pltpu.VMEM((t, half), F32),
                        pltpu.VMEM((t, half), BF16)],
        compiler_params=_params(1),
        name="gmlp",
    )(x, w_in.astype(BF16), b_in[None], n_g[None], n_b[None], ws, bs, w_out.astype(BF16),
      b_out[None], ln_g[None], ln_b[None])


def kernel(x, a_w_in, a_b_in, a_w_dw, a_b_dw, a_ln_g, a_ln_b, a_w_out, a_b_out, b_w_qkv, b_lq1, b_lk1, b_lq2, b_lk2, b_subln_g, b_w_o, c_w_in, c_b_in, c_ln_g, c_ln_b, c_w_s, c_b_s, c_w_out, c_b_out, f_w_up, f_b_up, f_w_dw, f_b_dw, f_w_down, f_b_down, ln_mix_g, ln_mix_b, ln_ffn_g, ln_ffn_b):
    bsz, seq, d = x.shape
    depth = f_w_up.shape[0]
    alpha = (2 * depth) ** 0.25
    x = _to_tile_order(x.reshape(bsz * seq, d), ROW_TILE)
    for i in range(depth):
        kind, j = i % N_MIXERS, i // N_MIXERS
        mix_ln = (ln_mix_g[i], ln_mix_b[i], alpha)
        if kind == 0:
            h = _glu(x, a_w_in[j], a_b_in[j])
            x = _conv_out(h, x, seq, a_w_dw[j], a_b_dw[j], a_ln_g[j], a_ln_b[j],
                          a_w_out[j], a_b_out[j], *mix_ln)
        elif kind == 1:
            lambda_init = 0.8 - 0.6 * math.exp(-0.3 * i)
            lam = (jnp.exp(jnp.sum(b_lq1[j] * b_lk1[j])) - jnp.exp(jnp.sum(b_lq2[j] * b_lk2[j]))
                   + lambda_init)
            q, k, v = _qkv(x, b_w_qkv[j])
            o = _attention(q, k, v, lam, b_subln_g[j], bsz, seq, 1.0 - lambda_init)
            x = _proj_ln(o, x, b_w_o[j], jnp.zeros((d,), F32), *mix_ln)
        else:
            x = _gmlp(x, c_w_in[j], c_b_in[j], c_ln_g[j], c_ln_b[j], c_w_s[j], c_b_s[j],
                      c_w_out[j], c_b_out[j], *mix_ln)
        x = _conv_ffn(x, seq, f_w_up[i], f_b_up[i], f_w_dw[i], f_b_dw[i],
                      f_w_down[i], f_b_down[i], ln_ffn_g[i], ln_ffn_b[i], alpha)
    return _from_tile_order(x, ROW_TILE).reshape(bsz, seq, d)
```

```python
import functools
import math

import jax
import jax.numpy as jnp
from jax import lax
from jax.experimental import pallas as pl
from jax.experimental.pallas import tpu as pltpu

F32 = jnp.float32
BF16 = jnp.bfloat16

N_MIXERS = 3
CONV_KERNEL = 31
DA_HEADS = 8
DA_HEAD_DIM = 64
HEAD_W = 2 * DA_HEAD_DIM
GMLP_GROUPS = 4
CHUNK = 128
FFN_CONV = 3
LN_EPS = 1e-5
RMS_EPS = 1e-5
LOG2_E = math.log2(math.e)

V7X_SUBLANES = 8
V7X_BF16_SUBLANES = 16
V7X_MXU_COLS = 256
V7X_VMEM_BYTES = 64 * 1024 * 1024
VMEM_LIMIT = V7X_VMEM_BYTES - 8 * 1024 * 1024

ROW_TILE = 512
CONV_HALO_BLOCKS = 32
CONV_STRIP = 64
ATT_Q_TILE = 512
ATT_K_TILE = 256
V_PAD = V7X_BF16_SUBLANES
COL_CHUNK = V7X_MXU_COLS


def _params(n_grid):
    return pltpu.CompilerParams(
        dimension_semantics=("arbitrary",) * n_grid, vmem_limit_bytes=VMEM_LIMIT)


def _const_spec(shape):
    return pl.BlockSpec(shape, lambda *_: (0,) * len(shape), pipeline_mode=pl.Buffered(1))


def _tile_order_positions(t):
    i = jnp.arange(t)
    return (i % V7X_SUBLANES) * (t // V7X_SUBLANES) + i // V7X_SUBLANES


def _to_tile_order(x, t):
    n, d = x.shape
    return x.reshape(n // t, V7X_SUBLANES, t // V7X_SUBLANES, d).transpose(0, 2, 1, 3).reshape(n, d)


def _from_tile_order(x, t):
    n, d = x.shape
    return x.reshape(n // t, t // V7X_SUBLANES, V7X_SUBLANES, d).transpose(0, 2, 1, 3).reshape(n, d)


def _segment_shift(cur, prev):
    n = cur.shape[0]
    sub = lax.broadcasted_iota(jnp.int32, cur.shape, 0) % V7X_SUBLANES
    return jnp.where(sub == 0, pltpu.roll(prev, n - (V7X_SUBLANES - 1), 0), pltpu.roll(cur, 1, 0))


def _layer_norm(r, g, b):
    mu = jnp.mean(r, axis=-1, keepdims=True)
    c = r - mu
    var = jnp.mean(c * c, axis=-1, keepdims=True)
    return c * lax.rsqrt(var + LN_EPS) * g + b


def _ffn_kernel(x_ref, wg_ref, wv_ref, bg_ref, bv_ref, cwg_ref, cwv_ref, cbg_ref, cbv_ref,
                wd_ref, bd_ref, lg_ref, lb_ref, o_ref, xb_ref, acc_ref, h_ref, act_ref,
                carry_ref, *, tiles_per_seq, n_chunks, alpha):
    first = (pl.program_id(0) % tiles_per_seq) == 0
    parity = pl.program_id(0) % 2
    rows = x_ref.shape[0]
    hist = (FFN_CONV - 1) * V7X_SUBLANES
    xb_ref[...] = x_ref[...].astype(BF16)

    def up(j, slot):
        xb = xb_ref[...]
        for half, (w_ref, b_ref) in enumerate(((wg_ref, bg_ref), (wv_ref, bv_ref))):
            h = jnp.dot(xb, w_ref[j], preferred_element_type=F32) + b_ref[j]
            h_ref[slot, half] = h
            carry_ref[parity, half, j] = h[rows - hist:]

    def gate(j, slot):
        conv = []
        for half, (cw_ref, cb_ref) in enumerate(((cwg_ref, cbg_ref), (cwv_ref, cbv_ref))):
            cw = cw_ref[j]
            h = h_ref[slot, half]
            prev = jnp.where(first, 0.0, carry_ref[1 - parity, half, j])
            hs = _segment_shift(h[rows - hist:], prev)
            back1 = jnp.concatenate([hs[V7X_SUBLANES:], h[:rows - V7X_SUBLANES]], axis=0)
            back2 = jnp.concatenate([hs, h[:rows - hist]], axis=0)
            conv.append(cw[0:1] * back2 + cw[1:2] * back1 + cw[2:3] * h + cb_ref[j])
        g, v = conv
        act_ref[slot] = ((g * jax.nn.sigmoid(g)) * v).astype(BF16)

    def down(j, slot):
        acc_ref[...] += jnp.dot(act_ref[slot], wd_ref[j], preferred_element_type=F32)

    def step(t, slot, has_down=True, has_up=True):
        if has_up:
            up(t + 1, 1 - slot)
        gate(t, slot)
        if has_down:
            down(t - 1, 1 - slot)

    assert n_chunks % 2 == 1 and n_chunks >= 3
    last = n_chunks - 1
    acc_ref[...] = jnp.zeros_like(acc_ref)
    up(0, 0)
    step(0, 0, has_down=False)
    step(1, 1)

    def pair_body(u, carry):
        step(2 * u, 0)
        step(2 * u + 1, 1)
        return carry

    lax.fori_loop(1, last // 2, pair_body, 0)
    step(last, 0, has_up=False)
    down(last, 0)
    r = alpha * x_ref[...] + (acc_ref[...] + bd_ref[...])
    o_ref[...] = _layer_norm(r, lg_ref[...], lb_ref[...])


def _conv_ffn(x, seq, w_up, b_up, w_dw, b_dw, w_down, b_down, ln_g, ln_b, alpha):
    n, d = x.shape
    f = w_down.shape[0]
    tf = COL_CHUNK
    nc = f // tf
    assert nc * tf == f and n % ROW_TILE == 0 and seq % ROW_TILE == 0

    def cols(a):
        return a.reshape(a.shape[0], nc, tf).transpose(1, 0, 2)

    wg, wv = cols(w_up[:, :f].astype(BF16)), cols(w_up[:, f:].astype(BF16))
    bg, bv = cols(b_up[None, :f]), cols(b_up[None, f:])
    cwg, cwv = cols(w_dw[:, :f]), cols(w_dw[:, f:])
    cbg, cbv = cols(b_dw[None, :f]), cols(b_dw[None, f:])
    wd = w_down.astype(BF16).reshape(nc, tf, d)
    row_spec = pl.BlockSpec((ROW_TILE, d), lambda i: (i, 0))
    kern = functools.partial(_ffn_kernel, tiles_per_seq=seq // ROW_TILE, n_chunks=nc, alpha=alpha)
    return pl.pallas_call(
        kern,
        grid=(n // ROW_TILE,),
        in_specs=[row_spec] + [_const_spec(a.shape) for a in (wg, wv, bg, bv, cwg, cwv, cbg, cbv, wd)]
        + [_const_spec((1, d))] * 3,
        out_specs=row_spec,
        out_shape=jax.ShapeDtypeStruct((n, d), F32),
        scratch_shapes=[pltpu.VMEM((ROW_TILE, d), BF16),
                        pltpu.VMEM((ROW_TILE, d), F32),
                        pltpu.VMEM((2, 2, ROW_TILE, tf), F32),
                        pltpu.VMEM((2, ROW_TILE, tf), BF16),
                        pltpu.VMEM((2, 2, nc, (FFN_CONV - 1) * V7X_SUBLANES, tf), F32)],
        compiler_params=_params(1),
        name="conv_ffn",
    )(x, wg, wv, bg, bv, cwg, cwv, cbg, cbv, wd, b_down[None], ln_g[None], ln_b[None])


def _proj_ln_kernel(y_ref, x_ref, w_ref, b_ref, lg_ref, lb_ref, o_ref, *, alpha):
    y = jnp.dot(y_ref[...].astype(BF16), w_ref[...], preferred_element_type=F32) + b_ref[...]
    o_ref[...] = _layer_norm(alpha * x_ref[...] + y, lg_ref[...], lb_ref[...])


def _proj_ln(y, x, w, b, ln_g, ln_b, alpha):
    n, d = x.shape
    k = y.shape[1]
    return pl.pallas_call(
        functools.partial(_proj_ln_kernel, alpha=alpha),
        grid=(n // ROW_TILE,),
        in_specs=[pl.BlockSpec((ROW_TILE, k), lambda i: (i, 0)),
                  pl.BlockSpec((ROW_TILE, d), lambda i: (i, 0)),
                  _const_spec((k, d))] + [_const_spec((1, d))] * 3,
        out_specs=pl.BlockSpec((ROW_TILE, d), lambda i: (i, 0)),
        out_shape=jax.ShapeDtypeStruct((n, d), F32),
        compiler_params=_params(1),
        name="proj_ln",
    )(y, x, w.astype(BF16), b[None], ln_g[None], ln_b[None])


def _glu_kernel(x_ref, wa_ref, wg_ref, ba_ref, bg_ref, o_ref):
    xb = x_ref[...].astype(BF16)
    for c in range(0, o_ref.shape[1], COL_CHUNK):
        sl = slice(c, c + COL_CHUNK)
        a = jnp.dot(xb, wa_ref[:, sl], preferred_element_type=F32) + ba_ref[:, sl]
        g = jnp.dot(xb, wg_ref[:, sl], preferred_element_type=F32) + bg_ref[:, sl]
        o_ref[:, sl] = a * jax.nn.sigmoid(g)


def _glu(x, w_in, b_in):
    n, d = x.shape
    w = w_in.astype(BF16)
    row_spec = pl.BlockSpec((ROW_TILE, d), lambda i: (i, 0))
    return pl.pallas_call(
        _glu_kernel,
        grid=(n // ROW_TILE,),
        in_specs=[row_spec, _const_spec((d, d)), _const_spec((d, d)),
                  _const_spec((1, d)), _const_spec((1, d))],
        out_specs=row_spec,
        out_shape=jax.ShapeDtypeStruct((n, d), F32),
        compiler_params=_params(1),
        name="glu_in",
    )(x, w[:, :d], w[:, d:], b_in[None, :d], b_in[None, d:])


def _conv_out_kernel(h_ref, halo_ref, x_ref, cw_ref, cb_ref, ng_ref, nb_ref, w_ref, b_ref,
                     lg_ref, lb_ref, o_ref, buf_ref, conv_ref, *, tiles_per_seq, alpha):
    first = (pl.program_id(0) % tiles_per_seq) == 0
    rows = h_ref.shape[0]
    hist = CONV_HALO_BLOCKS * V7X_SUBLANES
    prev = jnp.where(first, 0.0, halo_ref[...])
    buf_ref[:hist] = _segment_shift(h_ref[rows - hist:], prev)
    buf_ref[hist:] = h_ref[...]
    lead = CONV_HALO_BLOCKS - (CONV_KERNEL - 1)
    for c in range(0, h_ref.shape[1], COL_CHUNK):
        sl = slice(c, c + COL_CHUNK)
        cw = cw_ref[:, sl]

        def strip(i, carry):
            r0 = pl.multiple_of(i * CONV_STRIP, CONV_STRIP)
            acc = jnp.zeros((CONV_STRIP, COL_CHUNK), F32) + cb_ref[:, sl]
            for k in range(CONV_KERNEL):
                acc = acc + cw[k:k + 1] * buf_ref[pl.ds(r0 + (lead + k) * V7X_SUBLANES, CONV_STRIP), sl]
            conv_ref[pl.ds(r0, CONV_STRIP), sl] = acc
            return carry

        lax.fori_loop(0, rows // CONV_STRIP, strip, 0)
    hn = _layer_norm(conv_ref[...], ng_ref[...], nb_ref[...])
    hn = hn * jax.nn.sigmoid(hn)
    y = jnp.dot(hn.astype(BF16), w_ref[...], preferred_element_type=F32) + b_ref[...]
    o_ref[...] = _layer_norm(alpha * x_ref[...] + y, lg_ref[...], lb_ref[...])


def _conv_out(h, x, seq, w_dw, b_dw, n_g, n_b, w_out, b_out, ln_g, ln_b, alpha):
    n, d = x.shape
    t = ROW_TILE
    hist = CONV_HALO_BLOCKS * V7X_SUBLANES
    per_halo = t // hist
    row_spec = pl.BlockSpec((t, d), lambda i: (i, 0))
    halo_spec = pl.BlockSpec((hist, d), lambda i: (jnp.maximum(i * per_halo - 1, 0), 0))
    kern = functools.partial(_conv_out_kernel, tiles_per_seq=seq // t, alpha=alpha)
    return pl.pallas_call(
        kern,
        grid=(n // t,),
        in_specs=[row_spec, halo_spec, row_spec, _const_spec((CONV_KERNEL, d))]
        + [_const_spec((1, d))] * 3 + [_const_spec((d, d))] + [_const_spec((1, d))] * 3,
        out_specs=row_spec,
        out_shape=jax.ShapeDtypeStruct((n, d), F32),
        scratch_shapes=[pltpu.VMEM((hist + t, d), F32), pltpu.VMEM((t, d), F32)],
        compiler_params=_params(1),
        name="conv_out",
    )(h, h, x, w_dw, b_dw[None], n_g[None], n_b[None], w_out.astype(BF16), b_out[None],
      ln_g[None], ln_b[None])


def _qkv_kernel(x_ref, w_ref, q_ref, k_ref, v_ref, *, scale):
    xb = x_ref[...].astype(BF16)
    d = q_ref.shape[1]
    for idx, (ref, mul) in enumerate(((q_ref, scale), (k_ref, None), (v_ref, None))):
        for c in range(0, d, COL_CHUNK):
            y = jnp.dot(xb, w_ref[:, idx * d + c: idx * d + c + COL_CHUNK],
                        preferred_element_type=F32)
            if mul is not None:
                y = y * mul
            ref[:, c:c + COL_CHUNK] = y.astype(BF16)


def _qkv(x, w_qkv):
    n, d = x.shape
    row_spec = pl.BlockSpec((ROW_TILE, d), lambda i: (i, 0))
    out = jax.ShapeDtypeStruct((n, d), BF16)
    return pl.pallas_call(
        functools.partial(_qkv_kernel, scale=DA_HEAD_DIM ** -0.5 * LOG2_E),
        grid=(n // ROW_TILE,),
        in_specs=[row_spec, _const_spec((d, 3 * d))],
        out_specs=[row_spec] * 3,
        out_shape=[out] * 3,
        compiler_params=_params(1),
        name="qkv",
    )(x, w_qkv.astype(BF16))


def _attn_kernel(lam_ref, qt_ref, k_ref, vt_ref, g_ref, o_ref,
                 m_ref, acc_ref, s_ref, p_ref, a_ref, *, out_scale):
    qi = pl.program_id(2)
    tq = qt_ref.shape[-1]
    tk = vt_ref.shape[-1]
    per_q = tq // tk
    qt = qt_ref[0, 0, 0]
    comp = lax.broadcasted_iota(jnp.int32, qt.shape, 0) < DA_HEAD_DIM
    qts = (jnp.where(comp, qt, jnp.zeros_like(qt)), jnp.where(comp, jnp.zeros_like(qt), qt))

    def position(i):
        return (i % V7X_SUBLANES) * (tq // V7X_SUBLANES) + i // V7X_SUBLANES

    def scores(j, slot):
        kb = k_ref[0, pl.ds(pl.multiple_of(j * tk, tk), tk), :]
        for c in range(2):
            s_ref[slot, c] = jnp.dot(kb, qts[c], preferred_element_type=F32)

    def softmax(slot, diag):
        for c in range(2):
            s = s_ref[slot, c]
            if diag is not None:
                key = position(lax.broadcasted_iota(jnp.int32, s.shape, 0) + diag * tk)
                qry = position(lax.broadcasted_iota(jnp.int32, s.shape, 1))
                s = jnp.where(key <= qry, s, -jnp.inf)
            m_old = m_ref[c]
            m_new = jnp.maximum(m_old, jnp.max(s, axis=0, keepdims=True))
            m_ref[c] = m_new
            a_ref[slot, c] = jnp.exp2(m_old - m_new)
            p_ref[slot, c] = jnp.exp2(s - m_new).astype(BF16)

    def values(j, slot):
        vt = vt_ref[0, 0, j]
        for c in range(2):
            acc_ref[c] = a_ref[slot, c] * acc_ref[c] + jnp.dot(
                vt, p_ref[slot, c], preferred_element_type=F32)

    m_ref[...] = jnp.full_like(m_ref, -jnp.inf)
    acc_ref[...] = jnp.zeros_like(acc_ref)
    p_ref[1] = jnp.zeros_like(p_ref[1])
    a_ref[1] = jnp.ones_like(a_ref[1])

    first_diag = qi * per_q
    scores(0, 0)

    def pair(u, carry):
        j = 2 * u
        values(jnp.maximum(j - 1, 0), 1)
        softmax(0, None)
        scores(j + 1, 1)
        values(j, 0)
        softmax(1, None)
        scores(j + 2, 0)
        return carry

    assert per_q == 2 and tq == ROW_TILE
    lax.fori_loop(0, qi, pair, 0)
    values(jnp.maximum(first_diag - 1, 0), 1)
    softmax(0, 0)
    scores(first_diag + 1, 1)
    values(first_diag, 0)
    softmax(1, 1)
    values(first_diag + 1, 1)

    lam = lam_ref[0, 0]
    o = (acc_ref[0, :HEAD_W] / acc_ref[0, HEAD_W:HEAD_W + 1]
         - lam * (acc_ref[1, :HEAD_W] / acc_ref[1, HEAD_W:HEAD_W + 1]))
    o = o * lax.rsqrt(jnp.mean(o * o, axis=0, keepdims=True) + RMS_EPS) * g_ref[...] * out_scale
    o_ref[...] = o.T.astype(o_ref.dtype)


def _attention(q, k, v, lam, subln_g, bsz, seq, out_scale):
    n, d = q.shape
    tq, tk = ATT_Q_TILE, ATT_K_TILE
    nq, nk = seq // tq, seq // tk

    def transposed(a, t):
        return a.reshape(bsz, seq // t, t, DA_HEADS, HEAD_W).transpose(0, 3, 1, 4, 2)

    qt, vt = transposed(q, tq), transposed(v, tk)
    ones = jnp.zeros(vt.shape[:3] + (V_PAD, tk), BF16).at[..., 0, :].set(1)
    vt = jnp.concatenate([vt, ones], axis=3)
    k3 = k.reshape(bsz, seq, d)
    return pl.pallas_call(
        functools.partial(_attn_kernel, out_scale=out_scale),
        grid=(bsz, DA_HEADS, nq),
        in_specs=[pl.BlockSpec(memory_space=pltpu.SMEM),
                  pl.BlockSpec((1, 1, 1, HEAD_W, tq), lambda b, h, i: (b, h, i, 0, 0)),
                  pl.BlockSpec((1, seq, HEAD_W), lambda b, h, i: (b, 0, h)),
                  pl.BlockSpec((1, 1, nk, HEAD_W + V_PAD, tk), lambda b, h, i: (b, h, 0, 0, 0)),
                  pl.BlockSpec((HEAD_W, 1), lambda b, h, i: (0, 0))],
        out_specs=pl.BlockSpec((tq, HEAD_W), lambda b, h, i: (b * nq + i, h)),
        out_shape=jax.ShapeDtypeStruct((n, d), BF16),
        scratch_shapes=[pltpu.VMEM((2, 1, tq), F32),
                        pltpu.VMEM((2, HEAD_W + V_PAD, tq), F32),
                        pltpu.VMEM((2, 2, tk, tq), F32), pltpu.VMEM((2, 2, tk, tq), BF16),
                        pltpu.VMEM((2, 2, 1, tq), F32)],
        compiler_params=_params(3),
        name="diff_attention",
    )(lam.reshape(1, 1), qt, k3, vt, subln_g[:, None])


def _gmlp_kernel(x_ref, wi_ref, bi_ref, ng_ref, nb_ref, ws_ref, bs_ref, wo_ref, bo_ref,
                 lg_ref, lb_ref, o_ref, u_ref, v_ref, vn_ref, *, alpha):
    rows, half = u_ref.shape
    gw = half // GMLP_GROUPS
    xb = x_ref[...].astype(BF16)
    for c in range(0, 2 * half, COL_CHUNK):
        z = jnp.dot(xb, wi_ref[:, c:c + COL_CHUNK], preferred_element_type=F32) + bi_ref[:, c:c + COL_CHUNK]
        z = 0.5 * z * (1.0 + lax.erf(z * math.sqrt(0.5)))
        if c < half:
            u_ref[:, c:c + COL_CHUNK] = z
        else:
            v_ref[:, c - half:c - half + COL_CHUNK] = z
    for r0 in range(0, rows, CHUNK):
        vn_ref[r0:r0 + CHUNK] = _layer_norm(v_ref[r0:r0 + CHUNK], ng_ref[...], nb_ref[...]).astype(BF16)
    for g in range(GMLP_GROUPS):
        cs = slice(g * gw, (g + 1) * gw)
        sv = jnp.dot(ws_ref[g], vn_ref[:, cs], preferred_element_type=F32) + bs_ref[g]
        vn_ref[:, cs] = (u_ref[:, cs] * sv).astype(BF16)
    y = jnp.dot(vn_ref[...], wo_ref[...], preferred_element_type=F32) + bo_ref[...]
    o_ref[...] = _layer_norm(alpha * x_ref[...] + y, lg_ref[...], lb_ref[...])


def _spatial_weights(w_s, b_s, t):
    seg = t // V7X_SUBLANES
    per_chunk = CHUNK // seg
    n_chunks = V7X_SUBLANES // per_chunk
    g = w_s.shape[0]
    w = w_s.reshape(g, per_chunk, seg, per_chunk, seg).transpose(0, 2, 1, 4, 3)
    eye = jnp.eye(n_chunks, dtype=w_s.dtype)
    w = w[:, :, None, :, :, None, :] * eye[None, None, :, None, None, :, None]
    w = w.reshape(g, t, t)
    p = _tile_order_positions(t) % CHUNK
    w = jnp.where(p[None, :] <= p[:, None], w, 0.0)
    b = b_s.reshape(g, per_chunk, seg).transpose(0, 2, 1)[:, :, None, :]
    b = jnp.broadcast_to(b, (g, seg, n_chunks, per_chunk))
    return w.astype(BF16), b.reshape(g, t, 1)


def _gmlp(x, w_in, b_in, n_g, n_b, w_s, b_s, w_out, b_out, ln_g, ln_b, alpha):
    n, d = x.shape
    f = w_in.shape[1]
    half = f // 2
    t = ROW_TILE
    ws, bs = _spatial_weights(w_s, b_s, t)
    row_spec = pl.BlockSpec((t, d), lambda i: (i, 0))
    return pl.pallas_call(
        functools.partial(_gmlp_kernel, alpha=alpha),
        grid=(n // t,),
        in_specs=[row_spec, _const_spec((d, f)), _const_spec((1, f)),
                  _const_spec((1, half)), _const_spec((1, half)),
                  _const_spec((GMLP_GROUPS, t, t)), _const_spec((GMLP_GROUPS, t, 1)),
                  _const_spec((half, d))] + [_const_spec((1, d))] * 3,
        out_specs=row_spec,
        out_shape=jax.ShapeDtypeStruct((n, d), F32),
        scratch_shapes=[pltpu.VMEM((t, half), F32), pltpu.VMEM((t, half), F32),
                        pltpu.VMEM((t, half), BF16)],
        compiler_params=_params(1),
        name="gmlp",
    )(x, w_in.astype(BF16), b_in[None], n_g[None], n_b[None], ws, bs, w_out.astype(BF16),
      b_out[None], ln_g[None], ln_b[None])


def kernel(x, a_w_in, a_b_in, a_w_dw, a_b_dw, a_ln_g, a_ln_b, a_w_out, a_b_out, b_w_qkv, b_lq1, b_lk1, b_lq2, b_lk2, b_subln_g, b_w_o, c_w_in, c_b_in, c_ln_g, c_ln_b, c_w_s, c_b_s, c_w_out, c_b_out, f_w_up, f_b_up, f_w_dw, f_b_dw, f_w_down, f_b_down, ln_mix_g, ln_mix_b, ln_ffn_g, ln_ffn_b):
    bsz, seq, d = x.shape
    depth = f_w_up.shape[0]
    alpha = (2 * depth) ** 0.25
    x = _to_tile_order(x.reshape(bsz * seq, d), ROW_TILE)
    for i in range(depth):
        kind, j = i % N_MIXERS, i // N_MIXERS
        mix_ln = (ln_mix_g[i], ln_mix_b[i], alpha)
        if kind == 0:
            h = _glu(x, a_w_in[j], a_b_in[j])
            x = _conv_out(h, x, seq, a_w_dw[j], a_b_dw[j], a_ln_g[j], a_ln_b[j],
                          a_w_out[j], a_b_out[j], *mix_ln)
        elif kind == 1:
            lambda_init = 0.8 - 0.6 * math.exp(-0.3 * i)
            lam = (jnp.exp(jnp.sum(b_lq1[j] * b_lk1[j])) - jnp.exp(jnp.sum(b_lq2[j] * b_lk2[j]))
                   + lambda_init)
            q, k, v = _qkv(x, b_w_qkv[j])
            o = _attention(q, k, v, lam, b_subln_g[j], bsz, seq, 1.0 - lambda_init)
            x = _proj_ln(o, x, b_w_o[j], jnp.zeros((d,), F32), *mix_ln)
        else:
            x = _gmlp(x, c_w_in[j], c_b_in[j], c_ln_g[j], c_ln_b[j], c_w_s[j], c_b_s[j],
                      c_w_out[j], c_b_out[j], *mix_ln)
        x = _conv_ffn(x, seq, f_w_up[i], f_b_up[i], f_w_dw[i], f_b_dw[i],
                      f_w_down[i], f_b_down[i], ln_ffn_g[i], ln_ffn_b[i], alpha)
    return _from_tile_order(x, ROW_TILE).reshape(bsz, seq, d)
```

```python
import functools
import math

import jax
import jax.numpy as jnp
from jax import lax
from jax.experimental import pallas as pl
from jax.experimental.pallas import tpu as pltpu

F32 = jnp.float32
BF16 = jnp.bfloat16

N_MIXERS = 3
CONV_KERNEL = 31
DA_HEADS = 8
DA_HEAD_DIM = 64
HEAD_W = 2 * DA_HEAD_DIM
GMLP_GROUPS = 4
CHUNK = 128
FFN_CONV = 3
LN_EPS = 1e-5
RMS_EPS = 1e-5
LOG2_E = math.log2(math.e)

V7X_SUBLANES = 8
V7X_BF16_SUBLANES = 16
V7X_MXU_COLS = 256
V7X_VMEM_BYTES = 64 * 1024 * 1024
VMEM_LIMIT = V7X_VMEM_BYTES - 8 * 1024 * 1024

ROW_TILE = 512
CONV_HALO_BLOCKS = 32
CONV_STRIP = 64
ATT_Q_TILE = 512
ATT_K_TILE = 256
V_PAD = V7X_BF16_SUBLANES
COL_CHUNK = V7X_MXU_COLS


def _params(n_grid):
    return pltpu.CompilerParams(
        dimension_semantics=("arbitrary",) * n_grid, vmem_limit_bytes=VMEM_LIMIT)


def _const_spec(shape):
    return pl.BlockSpec(shape, lambda *_: (0,) * len(shape), pipeline_mode=pl.Buffered(1))


def _tile_order_positions(t):
    i = jnp.arange(t)
    return (i % V7X_SUBLANES) * (t // V7X_SUBLANES) + i // V7X_SUBLANES


def _to_tile_order(x, t):
    n, d = x.shape
    return x.reshape(n // t, V7X_SUBLANES, t // V7X_SUBLANES, d).transpose(0, 2, 1, 3).reshape(n, d)


def _from_tile_order(x, t):
    n, d = x.shape
    return x.reshape(n // t, t // V7X_SUBLANES, V7X_SUBLANES, d).transpose(0, 2, 1, 3).reshape(n, d)


def _segment_shift(cur, prev):
    n = cur.shape[0]
    sub = lax.broadcasted_iota(jnp.int32, cur.shape, 0) % V7X_SUBLANES
    return jnp.where(sub == 0, pltpu.roll(prev, n - (V7X_SUBLANES - 1), 0), pltpu.roll(cur, 1, 0))


def _layer_norm(r, g, b):
    mu = jnp.mean(r, axis=-1, keepdims=True)
    c = r - mu
    var = jnp.mean(c * c, axis=-1, keepdims=True)
    return c * lax.rsqrt(var + LN_EPS) * g + b


def _ffn_kernel(x_ref, wg_ref, wv_ref, bg_ref, bv_ref, cwg_ref, cwv_ref, cbg_ref, cbv_ref,
                wd_ref, bd_ref, lg_ref, lb_ref, o_ref, xb_ref, acc_ref, h_ref, act_ref,
                carry_ref, *, tiles_per_seq, n_chunks, alpha):
    first = (pl.program_id(0) % tiles_per_seq) == 0
    parity = pl.program_id(0) % 2
    rows = x_ref.shape[0]
    hist = (FFN_CONV - 1) * V7X_SUBLANES
    xb_ref[...] = x_ref[...].astype(BF16)

    def up(j, slot):
        xb = xb_ref[...]
        for half, (w_ref, b_ref) in enumerate(((wg_ref, bg_ref), (wv_ref, bv_ref))):
            h = jnp.dot(xb, w_ref[j], preferred_element_type=F32) + b_ref[j]
            h_ref[slot, half] = h
            carry_ref[parity, half, j] = h[rows - hist:]

    def gate(j, slot, act_slot):
        conv = []
        for half, (cw_ref, cb_ref) in enumerate(((cwg_ref, cbg_ref), (cwv_ref, cbv_ref))):
            cw = cw_ref[j]
            h = h_ref[slot, half]
            prev = jnp.where(first, 0.0, carry_ref[1 - parity, half, j])
            hs = _segment_shift(h[rows - hist:], prev)
            back1 = jnp.concatenate([hs[V7X_SUBLANES:], h[:rows - V7X_SUBLANES]], axis=0)
            back2 = jnp.concatenate([hs, h[:rows - hist]], axis=0)
            conv.append(cw[0:1] * back2 + cw[1:2] * back1 + cw[2:3] * h + cb_ref[j])
        g, v = conv
        act_ref[act_slot] = ((g * jax.nn.sigmoid(g)) * v).astype(BF16)

    def down(j, slot):
        acc_ref[...] += jnp.dot(act_ref[slot], wd_ref[j], preferred_element_type=F32)

    last = n_chunks - 1
    acc_ref[...] = jnp.zeros_like(acc_ref)
    up(0, 0)
    for t in range(n_chunks + 2):
        if t + 1 <= last:
            up(t + 1, (t + 1) % 2)
        if t <= last:
            gate(t, t % 2, t % 3)
        if 0 <= t - 2 <= last:
            down(t - 2, (t - 2) % 3)
    r = alpha * x_ref[...] + (acc_ref[...] + bd_ref[...])
    o_ref[...] = _layer_norm(r, lg_ref[...], lb_ref[...])


def _conv_ffn(x, seq, w_up, b_up, w_dw, b_dw, w_down, b_down, ln_g, ln_b, alpha):
    n, d = x.shape
    f = w_down.shape[0]
    tf = COL_CHUNK
    nc = f // tf
    assert nc * tf == f and n % ROW_TILE == 0 and seq % ROW_TILE == 0

    def cols(a):
        return a.reshape(a.shape[0], nc, tf).transpose(1, 0, 2)

    wg, wv = cols(w_up[:, :f].astype(BF16)), cols(w_up[:, f:].astype(BF16))
    bg, bv = cols(b_up[None, :f]), cols(b_up[None, f:])
    cwg, cwv = cols(w_dw[:, :f]), cols(w_dw[:, f:])
    cbg, cbv = cols(b_dw[None, :f]), cols(b_dw[None, f:])
    wd = w_down.astype(BF16).reshape(nc, tf, d)
    row_spec = pl.BlockSpec((ROW_TILE, d), lambda i: (i, 0))
    kern = functools.partial(_ffn_kernel, tiles_per_seq=seq // ROW_TILE, n_chunks=nc, alpha=alpha)
    return pl.pallas_call(
        kern,
        grid=(n // ROW_TILE,),
        in_specs=[row_spec] + [_const_spec(a.shape) for a in (wg, wv, bg, bv, cwg, cwv, cbg, cbv, wd)]
        + [_const_spec((1, d))] * 3,
        out_specs=row_spec,
        out_shape=jax.ShapeDtypeStruct((n, d), F32),
        scratch_shapes=[pltpu.VMEM((ROW_TILE, d), BF16),
                        pltpu.VMEM((ROW_TILE, d), F32),
                        pltpu.VMEM((2, 2, ROW_TILE, tf), F32),
                        pltpu.VMEM((3, ROW_TILE, tf), BF16),
                        pltpu.VMEM((2, 2, nc, (FFN_CONV - 1) * V7X_SUBLANES, tf), F32)],
        compiler_params=_params(1),
        name="conv_ffn",
    )(x, wg, wv, bg, bv, cwg, cwv, cbg, cbv, wd, b_down[None], ln_g[None], ln_b[None])


def _proj_ln_kernel(y_ref, x_ref, w_ref, b_ref, lg_ref, lb_ref, o_ref, *, alpha):
    y = jnp.dot(y_ref[...].astype(BF16), w_ref[...], preferred_element_type=F32) + b_ref[...]
    o_ref[...] = _layer_norm(alpha * x_ref[...] + y, lg_ref[...], lb_ref[...])


def _proj_ln(y, x, w, b, ln_g, ln_b, alpha):
    n, d = x.shape
    k = y.shape[1]
    return pl.pallas_call(
        functools.partial(_proj_ln_kernel, alpha=alpha),
        grid=(n // ROW_TILE,),
        in_specs=[pl.BlockSpec((ROW_TILE, k), lambda i: (i, 0)),
                  pl.BlockSpec((ROW_TILE, d), lambda i: (i, 0)),
                  _const_spec((k, d))] + [_const_spec((1, d))] * 3,
        out_specs=pl.BlockSpec((ROW_TILE, d), lambda i: (i, 0)),
        out_shape=jax.ShapeDtypeStruct((n, d), F32),
        compiler_params=_params(1),
        name="proj_ln",
    )(y, x, w.astype(BF16), b[None], ln_g[None], ln_b[None])


def _glu_kernel(x_ref, wa_ref, wg_ref, ba_ref, bg_ref, o_ref):
    xb = x_ref[...].astype(BF16)
    for c in range(0, o_ref.shape[1], COL_CHUNK):
        sl = slice(c, c + COL_CHUNK)
        a = jnp.dot(xb, wa_ref[:, sl], preferred_element_type=F32) + ba_ref[:, sl]
        g = jnp.dot(xb, wg_ref[:, sl], preferred_element_type=F32) + bg_ref[:, sl]
        o_ref[:, sl] = a * jax.nn.sigmoid(g)


def _glu(x, w_in, b_in):
    n, d = x.shape
    w = w_in.astype(BF16)
    row_spec = pl.BlockSpec((ROW_TILE, d), lambda i: (i, 0))
    return pl.pallas_call(
        _glu_kernel,
        grid=(n // ROW_TILE,),
        in_specs=[row_spec, _const_spec((d, d)), _const_spec((d, d)),
                  _const_spec((1, d)), _const_spec((1, d))],
        out_specs=row_spec,
        out_shape=jax.ShapeDtypeStruct((n, d), F32),
        compiler_params=_params(1),
        name="glu_in",
    )(x, w[:, :d], w[:, d:], b_in[None, :d], b_in[None, d:])


def _conv_out_kernel(h_ref, halo_ref, x_ref, cw_ref, cb_ref, ng_ref, nb_ref, w_ref, b_ref,
                     lg_ref, lb_ref, o_ref, buf_ref, conv_ref, *, tiles_per_seq, alpha):
    first = (pl.program_id(0) % tiles_per_seq) == 0
    rows = h_ref.shape[0]
    hist = CONV_HALO_BLOCKS * V7X_SUBLANES
    prev = jnp.where(first, 0.0, halo_ref[...])
    buf_ref[:hist] = _segment_shift(h_ref[rows - hist:], prev)
    buf_ref[hist:] = h_ref[...]
    lead = CONV_HALO_BLOCKS - (CONV_KERNEL - 1)
    for c in range(0, h_ref.shape[1], COL_CHUNK):
        sl = slice(c, c + COL_CHUNK)
        cw = cw_ref[:, sl]

        def strip(i, carry):
            r0 = pl.multiple_of(i * CONV_STRIP, CONV_STRIP)
            acc = jnp.zeros((CONV_STRIP, COL_CHUNK), F32) + cb_ref[:, sl]
            for k in range(CONV_KERNEL):
                acc = acc + cw[k:k + 1] * buf_ref[pl.ds(r0 + (lead + k) * V7X_SUBLANES, CONV_STRIP), sl]
            conv_ref[pl.ds(r0, CONV_STRIP), sl] = acc
            return carry

        lax.fori_loop(0, rows // CONV_STRIP, strip, 0)
    hn = _layer_norm(conv_ref[...], ng_ref[...], nb_ref[...])
    hn = hn * jax.nn.sigmoid(hn)
    y = jnp.dot(hn.astype(BF16), w_ref[...], preferred_element_type=F32) + b_ref[...]
    o_ref[...] = _layer_norm(alpha * x_ref[...] + y, lg_ref[...], lb_ref[...])


def _conv_out(h, x, seq, w_dw, b_dw, n_g, n_b, w_out, b_out, ln_g, ln_b, alpha):
    n, d = x.shape
    t = ROW_TILE
    hist = CONV_HALO_BLOCKS * V7X_SUBLANES
    per_halo = t // hist
    row_spec = pl.BlockSpec((t, d), lambda i: (i, 0))
    halo_spec = pl.BlockSpec((hist, d), lambda i: (jnp.maximum(i * per_halo - 1, 0), 0))
    kern = functools.partial(_conv_out_kernel, tiles_per_seq=seq // t, alpha=alpha)
    return pl.pallas_call(
        kern,
        grid=(n // t,),
        in_specs=[row_spec, halo_spec, row_spec, _const_spec((CONV_KERNEL, d))]
        + [_const_spec((1, d))] * 3 + [_const_spec((d, d))] + [_const_spec((1, d))] * 3,
        out_specs=row_spec,
        out_shape=jax.ShapeDtypeStruct((n, d), F32),
        scratch_shapes=[pltpu.VMEM((hist + t, d), F32), pltpu.VMEM((t, d), F32)],
        compiler_params=_params(1),
        name="conv_out",
    )(h, h, x, w_dw, b_dw[None], n_g[None], n_b[None], w_out.astype(BF16), b_out[None],
      ln_g[None], ln_b[None])


def _qkv_kernel(x_ref, w_ref, q_ref, k_ref, v_ref, *, scale):
    xb = x_ref[...].astype(BF16)
    d = q_ref.shape[1]
    for idx, (ref, mul) in enumerate(((q_ref, scale), (k_ref, None), (v_ref, None))):
        for c in range(0, d, COL_CHUNK):
            y = jnp.dot(xb, w_ref[:, idx * d + c: idx * d + c + COL_CHUNK],
                        preferred_element_type=F32)
            if mul is not None:
                y = y * mul
            ref[:, c:c + COL_CHUNK] = y.astype(BF16)


def _qkv(x, w_qkv):
    n, d = x.shape
    row_spec = pl.BlockSpec((ROW_TILE, d), lambda i: (i, 0))
    out = jax.ShapeDtypeStruct((n, d), BF16)
    return pl.pallas_call(
        functools.partial(_qkv_kernel, scale=DA_HEAD_DIM ** -0.5 * LOG2_E),
        grid=(n // ROW_TILE,),
        in_specs=[row_spec, _const_spec((d, 3 * d))],
        out_specs=[row_spec] * 3,
        out_shape=[out] * 3,
        compiler_params=_params(1),
        name="qkv",
    )(x, w_qkv.astype(BF16))


def _attn_kernel(lam_ref, qt_ref, k_ref, vt_ref, g_ref, o_ref,
                 m_ref, acc_ref, s_ref, p_ref, a_ref, *, out_scale):
    qi = pl.program_id(2)
    tq = qt_ref.shape[-1]
    tk = vt_ref.shape[-1]
    per_q = tq // tk
    qt = qt_ref[0, 0, 0]
    comp = lax.broadcasted_iota(jnp.int32, qt.shape, 0) < DA_HEAD_DIM
    qts = (jnp.where(comp, qt, jnp.zeros_like(qt)), jnp.where(comp, jnp.zeros_like(qt), qt))

    def position(i):
        return (i % V7X_SUBLANES) * (tq // V7X_SUBLANES) + i // V7X_SUBLANES

    def scores(j, slot):
        kb = k_ref[0, pl.ds(pl.multiple_of(j * tk, tk), tk), :]
        for c in range(2):
            s_ref[slot, c] = jnp.dot(kb, qts[c], preferred_element_type=F32)

    def softmax(slot, diag):
        for c in range(2):
            s = s_ref[slot, c]
            if diag is not None:
                key = position(lax.broadcasted_iota(jnp.int32, s.shape, 0) + diag * tk)
                qry = position(lax.broadcasted_iota(jnp.int32, s.shape, 1))
                s = jnp.where(key <= qry, s, -jnp.inf)
            m_old = m_ref[c]
            m_new = jnp.maximum(m_old, jnp.max(s, axis=0, keepdims=True))
            m_ref[c] = m_new
            a_ref[slot, c] = jnp.exp2(m_old - m_new)
            p_ref[slot, c] = jnp.exp2(s - m_new).astype(BF16)

    def values(j, slot):
        vt = vt_ref[0, 0, j]
        for c in range(2):
            acc_ref[c] = a_ref[slot, c] * acc_ref[c] + jnp.dot(
                vt, p_ref[slot, c], preferred_element_type=F32)

    m_ref[...] = jnp.full_like(m_ref, -jnp.inf)
    acc_ref[...] = jnp.zeros_like(acc_ref)
    p_ref[1] = jnp.zeros_like(p_ref[1])
    a_ref[1] = jnp.ones_like(a_ref[1])

    first_diag = qi * per_q
    scores(0, 0)

    def pair(u, carry):
        j = 2 * u
        values(jnp.maximum(j - 1, 0), 1)
        softmax(0, None)
        scores(j + 1, 1)
        values(j, 0)
        softmax(1, None)
        scores(j + 2, 0)
        return carry

    assert per_q == 2 and tq == ROW_TILE
    lax.fori_loop(0, qi, pair, 0)
    values(jnp.maximum(first_diag - 1, 0), 1)
    softmax(0, 0)
    scores(first_diag + 1, 1)
    values(first_diag, 0)
    softmax(1, 1)
    values(first_diag + 1, 1)

    lam = lam_ref[0, 0]
    o = (acc_ref[0, :HEAD_W] / acc_ref[0, HEAD_W:HEAD_W + 1]
         - lam * (acc_ref[1, :HEAD_W] / acc_ref[1, HEAD_W:HEAD_W + 1]))
    o = o * lax.rsqrt(jnp.mean(o * o, axis=0, keepdims=True) + RMS_EPS) * g_ref[...] * out_scale
    o_ref[...] = o.T.astype(o_ref.dtype)


def _attention(q, k, v, lam, subln_g, bsz, seq, out_scale):
    n, d = q.shape
    tq, tk = ATT_Q_TILE, ATT_K_TILE
    nq, nk = seq // tq, seq // tk

    def transposed(a, t):
        return a.reshape(bsz, seq // t, t, DA_HEADS, HEAD_W).transpose(0, 3, 1, 4, 2)

    qt, vt = transposed(q, tq), transposed(v, tk)
    ones = jnp.zeros(vt.shape[:3] + (V_PAD, tk), BF16).at[..., 0, :].set(1)
    vt = jnp.concatenate([vt, ones], axis=3)
    k3 = k.reshape(bsz, seq, d)
    return pl.pallas_call(
        functools.partial(_attn_kernel, out_scale=out_scale),
        grid=(bsz, DA_HEADS, nq),
        in_specs=[pl.BlockSpec(memory_space=pltpu.SMEM),
                  pl.BlockSpec((1, 1, 1, HEAD_W, tq), lambda b, h, i: (b, h, i, 0, 0)),
                  pl.BlockSpec((1, seq, HEAD_W), lambda b, h, i: (b, 0, h)),
                  pl.BlockSpec((1, 1, nk, HEAD_W + V_PAD, tk), lambda b, h, i: (b, h, 0, 0, 0)),
                  pl.BlockSpec((HEAD_W, 1), lambda b, h, i: (0, 0))],
        out_specs=pl.BlockSpec((tq, HEAD_W), lambda b, h, i: (b * nq + i, h)),
        out_shape=jax.ShapeDtypeStruct((n, d), BF16),
        scratch_shapes=[pltpu.VMEM((2, 1, tq), F32),
                        pltpu.VMEM((2, HEAD_W + V_PAD, tq), F32),
                        pltpu.VMEM((2, 2, tk, tq), F32), pltpu.VMEM((2, 2, tk, tq), BF16),
                        pltpu.VMEM((2, 2, 1, tq), F32)],
        compiler_params=_params(3),
        name="diff_attention",
    )(lam.reshape(1, 1), qt, k3, vt, subln_g[:, None])


def _gmlp_kernel(x_ref, wi_ref, bi_ref, ng_ref, nb_ref, ws_ref, bs_ref, wo_ref, bo_ref,
                 lg_ref, lb_ref, o_ref, u_ref, v_ref, vn_ref, *, alpha):
    rows, half = u_ref.shape
    gw = half // GMLP_GROUPS
    xb = x_ref[...].astype(BF16)
    for c in range(0, 2 * half, COL_CHUNK):
        z = jnp.dot(xb, wi_ref[:, c:c + COL_CHUNK], preferred_element_type=F32) + bi_ref[:, c:c + COL_CHUNK]
        z = 0.5 * z * (1.0 + lax.erf(z * math.sqrt(0.5)))
        if c < half:
            u_ref[:, c:c + COL_CHUNK] = z
        else:
            v_ref[:, c - half:c - half + COL_CHUNK] = z
    for r0 in range(0, rows, CHUNK):
        vn_ref[r0:r0 + CHUNK] = _layer_norm(v_ref[r0:r0 + CHUNK], ng_ref[...], nb_ref[...]).astype(BF16)
    for g in range(GMLP_GROUPS):
        cs = slice(g * gw, (g + 1) * gw)
        sv = jnp.dot(ws_ref[g], vn_ref[:, cs], preferred_element_type=F32) + bs_ref[g]
        vn_ref[:, cs] = (u_ref[:, cs] * sv).astype(BF16)
    y = jnp.dot(vn_ref[...], wo_ref[...], preferred_element_type=F32) + bo_ref[...]
    o_ref[...] = _layer_norm(alpha * x_ref[...] + y, lg_ref[...], lb_ref[...])


def _spatial_weights(w_s, b_s, t):
    pos = _tile_order_positions(t)
    chunk, p = pos // CHUNK, pos % CHUNK
    onehot = (p[:, None] == jnp.arange(CHUNK)[None, :]).astype(w_s.dtype)
    w = jnp.einsum("ip,gpq,jq->gij", onehot, w_s.astype(BF16).astype(w_s.dtype), onehot)
    keep = (chunk[:, None] == chunk[None, :]) & (p[None, :] <= p[:, None])
    w = jnp.where(keep[None], w, 0.0).astype(BF16)
    b = jnp.einsum("ip,gp->gi", onehot, b_s, precision=lax.Precision.HIGHEST)
    return w, b[:, :, None]


def _gmlp(x, w_in, b_in, n_g, n_b, w_s, b_s, w_out, b_out, ln_g, ln_b, alpha):
    n, d = x.shape
    f = w_in.shape[1]
    half = f // 2
    t = ROW_TILE
    ws, bs = _spatial_weights(w_s, b_s, t)
    row_spec = pl.BlockSpec((t, d), lambda i: (i, 0))
    return pl.pallas_call(
        functools.partial(_gmlp_kernel, alpha=alpha),
        grid=(n // t,),
        in_specs=[row_spec, _const_spec((d, f)), _const_spec((1, f)),
                  _const_spec((1, half)), _const_spec((1, half)),
                  _const_spec((GMLP_GROUPS, t, t)), _const_spec((GMLP_GROUPS, t, 1)),
                  _const_spec((half, d))] + [_const_spec((1, d))] * 3,
        out_specs=row_spec,
        out_shape=jax.ShapeDtypeStruct((n, d), F32),
        scratch_shapes=[pltpu.VMEM((t, half), F32), pltpu.VMEM((t, half), F32),
                        pltpu.VMEM((t, half), BF16)],
        compiler_params=_params(1),
        name="gmlp",
    )(x, w_in.astype(BF16), b_in[None], n_g[None], n_b[None], ws, bs, w_out.astype(BF16),
      b_out[None], ln_g[None], ln_b[None])


def kernel(x, a_w_in, a_b_in, a_w_dw, a_b_dw, a_ln_g, a_ln_b, a_w_out, a_b_out, b_w_qkv, b_lq1, b_lk1, b_lq2, b_lk2, b_subln_g, b_w_o, c_w_in, c_b_in, c_ln_g, c_ln_b, c_w_s, c_b_s, c_w_out, c_b_out, f_w_up, f_b_up, f_w_dw, f_b_dw, f_w_down, f_b_down, ln_mix_g, ln_mix_b, ln_ffn_g, ln_ffn_b):
    bsz, seq, d = x.shape
    depth = f_w_up.shape[0]
    alpha = (2 * depth) ** 0.25
    x = _to_tile_order(x.reshape(bsz * seq, d), ROW_TILE)
    for i in range(depth):
        kind, j = i % N_MIXERS, i // N_MIXERS
        mix_ln = (ln_mix_g[i], ln_mix_b[i], alpha)
        if kind == 0:
            h = _glu(x, a_w_in[j], a_b_in[j])
            x = _conv_out(h, x, seq, a_w_dw[j], a_b_dw[j], a_ln_g[j], a_ln_b[j],
                          a_w_out[j], a_b_out[j], *mix_ln)
        elif kind == 1:
            lambda_init = 0.8 - 0.6 * math.exp(-0.3 * i)
            lam = (jnp.exp(jnp.sum(b_lq1[j] * b_lk1[j])) - jnp.exp(jnp.sum(b_lq2[j] * b_lk2[j]))
                   + lambda_init)
            q, k, v = _qkv(x, b_w_qkv[j])
            o = _attention(q, k, v, lam, b_subln_g[j], bsz, seq, 1.0 - lambda_init)
            x = _proj_ln(o, x, b_w_o[j], jnp.zeros((d,), F32), *mix_ln)
        else:
            x = _gmlp(x, c_w_in[j], c_b_in[j], c_ln_g[j], c_ln_b[j], c_w_s[j], c_b_s[j],
                      c_w_out[j], c_b_out[j], *mix_ln)
        x = _conv_ffn(x, seq, f_w_up[i], f_b_up[i], f_w_dw[i], f_b_dw[i],
                      f_w_down[i], f_b_down[i], ln_ffn_g[i], ln_ffn_b[i], alpha)
    return _from_tile_order(x, ROW_TILE).reshape(bsz, seq, d)
```

```python
import functools
import math

import jax
import jax.numpy as jnp
from jax import lax
from jax.experimental import pallas as pl
from jax.experimental.pallas import tpu as pltpu

F32 = jnp.float32
BF16 = jnp.bfloat16

N_MIXERS = 3
CONV_KERNEL = 31
DA_HEADS = 8
DA_HEAD_DIM = 64
HEAD_W = 2 * DA_HEAD_DIM
GMLP_GROUPS = 4
CHUNK = 128
FFN_CONV = 3
LN_EPS = 1e-5
RMS_EPS = 1e-5
LOG2_E = math.log2(math.e)

V7X_SUBLANES = 8
V7X_BF16_SUBLANES = 16
V7X_MXU_COLS = 256
V7X_VMEM_BYTES = 64 * 1024 * 1024
VMEM_LIMIT = V7X_VMEM_BYTES - 8 * 1024 * 1024

ROW_TILE = 512
CONV_HALO_BLOCKS = 32
CONV_STRIP = 64
ATT_Q_TILE = 512
ATT_K_TILE = 256
V_PAD = V7X_BF16_SUBLANES
COL_CHUNK = V7X_MXU_COLS


def _params(n_grid):
    return pltpu.CompilerParams(
        dimension_semantics=("arbitrary",) * n_grid, vmem_limit_bytes=VMEM_LIMIT)


def _const_spec(shape):
    return pl.BlockSpec(shape, lambda *_: (0,) * len(shape), pipeline_mode=pl.Buffered(1))


def _tile_order_positions(t):
    i = jnp.arange(t)
    return (i % V7X_SUBLANES) * (t // V7X_SUBLANES) + i // V7X_SUBLANES


def _to_tile_order(x, t):
    n, d = x.shape
    return x.reshape(n // t, V7X_SUBLANES, t // V7X_SUBLANES, d).transpose(0, 2, 1, 3).reshape(n, d)


def _from_tile_order(x, t):
    n, d = x.shape
    return x.reshape(n // t, t // V7X_SUBLANES, V7X_SUBLANES, d).transpose(0, 2, 1, 3).reshape(n, d)


def _segment_shift(cur, prev):
    n = cur.shape[0]
    sub = lax.broadcasted_iota(jnp.int32, cur.shape, 0) % V7X_SUBLANES
    return jnp.where(sub == 0, pltpu.roll(prev, n - (V7X_SUBLANES - 1), 0), pltpu.roll(cur, 1, 0))


def _layer_norm(r, g, b):
    mu = jnp.mean(r, axis=-1, keepdims=True)
    c = r - mu
    var = jnp.mean(c * c, axis=-1, keepdims=True)
    return c * lax.rsqrt(var + LN_EPS) * g + b


def _ffn_kernel(x_ref, wg_ref, wv_ref, bg_ref, bv_ref, cwg_ref, cwv_ref, cbg_ref, cbv_ref,
                wd_ref, bd_ref, lg_ref, lb_ref, o_ref, xb_ref, acc_ref, h_ref, act_ref,
                carry_ref, *, tiles_per_seq, n_chunks, alpha):
    first = (pl.program_id(0) % tiles_per_seq) == 0
    parity = pl.program_id(0) % 2
    rows = x_ref.shape[0]
    hist = (FFN_CONV - 1) * V7X_SUBLANES
    xb_ref[...] = x_ref[...].astype(BF16)

    def up(j, slot):
        xb = xb_ref[...]
        for half, (w_ref, b_ref) in enumerate(((wg_ref, bg_ref), (wv_ref, bv_ref))):
            h = jnp.dot(xb, w_ref[j], preferred_element_type=F32) + b_ref[j]
            h_ref[slot, half] = h
            carry_ref[parity, half, j] = h[rows - hist:]

    def gate(j, slot, act_slot):
        conv = []
        for half, (cw_ref, cb_ref) in enumerate(((cwg_ref, cbg_ref), (cwv_ref, cbv_ref))):
            cw = cw_ref[j]
            h = h_ref[slot, half]
            prev = jnp.where(first, 0.0, carry_ref[1 - parity, half, j])
            hs = _segment_shift(h[rows - hist:], prev)
            back1 = jnp.concatenate([hs[V7X_SUBLANES:], h[:rows - V7X_SUBLANES]], axis=0)
            back2 = jnp.concatenate([hs, h[:rows - hist]], axis=0)
            conv.append(cw[0:1] * back2 + cw[1:2] * back1 + cw[2:3] * h + cb_ref[j])
        g, v = conv
        act_ref[act_slot] = ((g * jax.nn.sigmoid(g)) * v).astype(BF16)

    def down(j, slot):
        acc_ref[...] += jnp.dot(act_ref[slot], wd_ref[j], preferred_element_type=F32)

    last = n_chunks - 1
    acc_ref[...] = jnp.zeros_like(acc_ref)
    up(0, 0)
    for t in range(n_chunks + 2):
        if t + 1 <= last:
            up(t + 1, (t + 1) % 2)
        if t <= last:
            gate(t, t % 2, t % 3)
        if 0 <= t - 2 <= last:
            down(t - 2, (t - 2) % 3)
    r = alpha * x_ref[...] + (acc_ref[...] + bd_ref[...])
    o_ref[...] = _layer_norm(r, lg_ref[...], lb_ref[...])


def _conv_ffn(x, seq, w_up, b_up, w_dw, b_dw, w_down, b_down, ln_g, ln_b, alpha):
    n, d = x.shape
    f = w_down.shape[0]
    tf = COL_CHUNK
    nc = f // tf
    assert nc * tf == f and n % ROW_TILE == 0 and seq % ROW_TILE == 0

    def cols(a):
        return a.reshape(a.shape[0], nc, tf).transpose(1, 0, 2)

    wg, wv = cols(w_up[:, :f].astype(BF16)), cols(w_up[:, f:].astype(BF16))
    bg, bv = cols(b_up[None, :f]), cols(b_up[None, f:])
    cwg, cwv = cols(w_dw[:, :f]), cols(w_dw[:, f:])
    cbg, cbv = cols(b_dw[None, :f]), cols(b_dw[None, f:])
    wd = w_down.astype(BF16).reshape(nc, tf, d)
    row_spec = pl.BlockSpec((ROW_TILE, d), lambda i: (i, 0))
    kern = functools.partial(_ffn_kernel, tiles_per_seq=seq // ROW_TILE, n_chunks=nc, alpha=alpha)
    return pl.pallas_call(
        kern,
        grid=(n // ROW_TILE,),
        in_specs=[row_spec] + [_const_spec(a.shape) for a in (wg, wv, bg, bv, cwg, cwv, cbg, cbv, wd)]
        + [_const_spec((1, d))] * 3,
        out_specs=row_spec,
        out_shape=jax.ShapeDtypeStruct((n, d), F32),
        scratch_shapes=[pltpu.VMEM((ROW_TILE, d), BF16),
                        pltpu.VMEM((ROW_TILE, d), F32),
                        pltpu.VMEM((2, 2, ROW_TILE, tf), F32),
                        pltpu.VMEM((3, ROW_TILE, tf), BF16),
                        pltpu.VMEM((2, 2, nc, (FFN_CONV - 1) * V7X_SUBLANES, tf), F32)],
        compiler_params=_params(1),
        name="conv_ffn",
    )(x, wg, wv, bg, bv, cwg, cwv, cbg, cbv, wd, b_down[None], ln_g[None], ln_b[None])


def _proj_ln_kernel(y_ref, x_ref, w_ref, b_ref, lg_ref, lb_ref, o_ref, *, alpha):
    y = jnp.dot(y_ref[...].astype(BF16), w_ref[...], preferred_element_type=F32) + b_ref[...]
    o_ref[...] = _layer_norm(alpha * x_ref[...] + y, lg_ref[...], lb_ref[...])


def _proj_ln(y, x, w, b, ln_g, ln_b, alpha):
    n, d = x.shape
    k = y.shape[1]
    return pl.pallas_call(
        functools.partial(_proj_ln_kernel, alpha=alpha),
        grid=(n // ROW_TILE,),
        in_specs=[pl.BlockSpec((ROW_TILE, k), lambda i: (i, 0)),
                  pl.BlockSpec((ROW_TILE, d), lambda i: (i, 0)),
                  _const_spec((k, d))] + [_const_spec((1, d))] * 3,
        out_specs=pl.BlockSpec((ROW_TILE, d), lambda i: (i, 0)),
        out_shape=jax.ShapeDtypeStruct((n, d), F32),
        compiler_params=_params(1),
        name="proj_ln",
    )(y, x, w.astype(BF16), b[None], ln_g[None], ln_b[None])


def _glu_kernel(x_ref, wa_ref, wg_ref, ba_ref, bg_ref, o_ref):
    xb = x_ref[...].astype(BF16)
    for c in range(0, o_ref.shape[1], COL_CHUNK):
        sl = slice(c, c + COL_CHUNK)
        a = jnp.dot(xb, wa_ref[:, sl], preferred_element_type=F32) + ba_ref[:, sl]
        g = jnp.dot(xb, wg_ref[:, sl], preferred_element_type=F32) + bg_ref[:, sl]
        o_ref[:, sl] = a * jax.nn.sigmoid(g)


def _glu(x, w_in, b_in):
    n, d = x.shape
    w = w_in.astype(BF16)
    row_spec = pl.BlockSpec((ROW_TILE, d), lambda i: (i, 0))
    return pl.pallas_call(
        _glu_kernel,
        grid=(n // ROW_TILE,),
        in_specs=[row_spec, _const_spec((d, d)), _const_spec((d, d)),
                  _const_spec((1, d)), _const_spec((1, d))],
        out_specs=row_spec,
        out_shape=jax.ShapeDtypeStruct((n, d), F32),
        compiler_params=_params(1),
        name="glu_in",
    )(x, w[:, :d], w[:, d:], b_in[None, :d], b_in[None, d:])


def _conv_out_kernel(h_ref, halo_ref, x_ref, cw_ref, cb_ref, ng_ref, nb_ref, w_ref, b_ref,
                     lg_ref, lb_ref, o_ref, buf_ref, conv_ref, *, tiles_per_seq, alpha):
    first = (pl.program_id(0) % tiles_per_seq) == 0
    rows = h_ref.shape[0]
    hist = CONV_HALO_BLOCKS * V7X_SUBLANES
    prev = jnp.where(first, 0.0, halo_ref[...])
    buf_ref[:hist] = _segment_shift(h_ref[rows - hist:], prev)
    buf_ref[hist:] = h_ref[...]
    lead = CONV_HALO_BLOCKS - (CONV_KERNEL - 1)
    for c in range(0, h_ref.shape[1], COL_CHUNK):
        sl = slice(c, c + COL_CHUNK)
        cw = cw_ref[:, sl]

        def strip(i, carry):
            r0 = pl.multiple_of(i * CONV_STRIP, CONV_STRIP)
            acc = jnp.zeros((CONV_STRIP, COL_CHUNK), F32) + cb_ref[:, sl]
            for k in range(CONV_KERNEL):
                acc = acc + cw[k:k + 1] * buf_ref[pl.ds(r0 + (lead + k) * V7X_SUBLANES, CONV_STRIP), sl]
            conv_ref[pl.ds(r0, CONV_STRIP), sl] = acc
            return carry

        lax.fori_loop(0, rows // CONV_STRIP, strip, 0)
    hn = _layer_norm(conv_ref[...], ng_ref[...], nb_ref[...])
    hn = hn * jax.nn.sigmoid(hn)
    y = jnp.dot(hn.astype(BF16), w_ref[...], preferred_element_type=F32) + b_ref[...]
    o_ref[...] = _layer_norm(alpha * x_ref[...] + y, lg_ref[...], lb_ref[...])


def _conv_out(h, x, seq, w_dw, b_dw, n_g, n_b, w_out, b_out, ln_g, ln_b, alpha):
    n, d = x.shape
    t = ROW_TILE
    hist = CONV_HALO_BLOCKS * V7X_SUBLANES
    per_halo = t // hist
    row_spec = pl.BlockSpec((t, d), lambda i: (i, 0))
    halo_spec = pl.BlockSpec((hist, d), lambda i: (jnp.maximum(i * per_halo - 1, 0), 0))
    kern = functools.partial(_conv_out_kernel, tiles_per_seq=seq // t, alpha=alpha)
    return pl.pallas_call(
        kern,
        grid=(n // t,),
        in_specs=[row_spec, halo_spec, row_spec, _const_spec((CONV_KERNEL, d))]
        + [_const_spec((1, d))] * 3 + [_const_spec((d, d))] + [_const_spec((1, d))] * 3,
        out_specs=row_spec,
        out_shape=jax.ShapeDtypeStruct((n, d), F32),
        scratch_shapes=[pltpu.VMEM((hist + t, d), F32), pltpu.VMEM((t, d), F32)],
        compiler_params=_params(1),
        name="conv_out",
    )(h, h, x, w_dw, b_dw[None], n_g[None], n_b[None], w_out.astype(BF16), b_out[None],
      ln_g[None], ln_b[None])


def _qkv_kernel(x_ref, w_ref, qt_ref, k_ref, vt_ref, *, scale):
    xb = x_ref[...].astype(BF16)
    rows, d = k_ref.shape
    tk = vt_ref.shape[-1]
    pad_row = lax.broadcasted_iota(jnp.int32, (V_PAD, tk), 0)
    ones_pad = jnp.where(pad_row == 0, 1.0, 0.0).astype(BF16)
    for c in range(0, d, COL_CHUNK):
        q = jnp.dot(xb, w_ref[:, c:c + COL_CHUNK], preferred_element_type=F32) * scale
        k = jnp.dot(xb, w_ref[:, d + c:d + c + COL_CHUNK], preferred_element_type=F32)
        v = jnp.dot(xb, w_ref[:, 2 * d + c:2 * d + c + COL_CHUNK], preferred_element_type=F32)
        k_ref[:, c:c + COL_CHUNK] = k.astype(BF16)
        for hh in range(COL_CHUNK // HEAD_W):
            head = c // HEAD_W + hh
            hs = slice(hh * HEAD_W, (hh + 1) * HEAD_W)
            qt_ref[0, head, 0] = q[:, hs].T.astype(BF16)
            for kb in range(rows // tk):
                vt_ref[0, head, kb, :HEAD_W] = v[kb * tk:(kb + 1) * tk, hs].T.astype(BF16)
                vt_ref[0, head, kb, HEAD_W:] = ones_pad


def _qkv(x, w_qkv, bsz, seq):
    n, d = x.shape
    tq, tk = ATT_Q_TILE, ATT_K_TILE
    assert tq == ROW_TILE
    nq, nk = seq // tq, seq // tk
    row_spec = pl.BlockSpec((ROW_TILE, d), lambda i: (i, 0))
    return pl.pallas_call(
        functools.partial(_qkv_kernel, scale=DA_HEAD_DIM ** -0.5 * LOG2_E),
        grid=(n // ROW_TILE,),
        in_specs=[row_spec, _const_spec((d, 3 * d))],
        out_specs=[pl.BlockSpec((1, DA_HEADS, 1, HEAD_W, tq), lambda i: (i // nq, 0, i % nq, 0, 0)),
                   row_spec,
                   pl.BlockSpec((1, DA_HEADS, tq // tk, HEAD_W + V_PAD, tk),
                                lambda i: (i // nq, 0, i % nq, 0, 0))],
        out_shape=[jax.ShapeDtypeStruct((bsz, DA_HEADS, nq, HEAD_W, tq), BF16),
                   jax.ShapeDtypeStruct((n, d), BF16),
                   jax.ShapeDtypeStruct((bsz, DA_HEADS, nk, HEAD_W + V_PAD, tk), BF16)],
        compiler_params=_params(1),
        name="qkv",
    )(x, w_qkv.astype(BF16))


def _attn_kernel(lam_ref, qt_ref, k_ref, vt_ref, g_ref, o_ref,
                 m_ref, acc_ref, s_ref, p_ref, a_ref, *, out_scale):
    qi = pl.program_id(2)
    tq = qt_ref.shape[-1]
    tk = vt_ref.shape[-1]
    per_q = tq // tk
    qt = qt_ref[0, 0, 0]
    comp = lax.broadcasted_iota(jnp.int32, qt.shape, 0) < DA_HEAD_DIM
    qts = (jnp.where(comp, qt, jnp.zeros_like(qt)), jnp.where(comp, jnp.zeros_like(qt), qt))

    def position(i):
        return (i % V7X_SUBLANES) * (tq // V7X_SUBLANES) + i // V7X_SUBLANES

    def scores(j, slot):
        kb = k_ref[0, pl.ds(pl.multiple_of(j * tk, tk), tk), :]
        for c in range(2):
            s_ref[slot, c] = jnp.dot(kb, qts[c], preferred_element_type=F32)

    def softmax(slot, diag):
        for c in range(2):
            s = s_ref[slot, c]
            if diag is not None:
                key = position(lax.broadcasted_iota(jnp.int32, s.shape, 0) + diag * tk)
                qry = position(lax.broadcasted_iota(jnp.int32, s.shape, 1))
                s = jnp.where(key <= qry, s, -jnp.inf)
            m_old = m_ref[c]
            m_new = jnp.maximum(m_old, jnp.max(s, axis=0, keepdims=True))
            m_ref[c] = m_new
            a_ref[slot, c] = jnp.exp2(m_old - m_new)
            p_ref[slot, c] = jnp.exp2(s - m_new).astype(BF16)

    def values(j, slot):
        vt = vt_ref[0, 0, j]
        for c in range(2):
            acc_ref[c] = a_ref[slot, c] * acc_ref[c] + jnp.dot(
                vt, p_ref[slot, c], preferred_element_type=F32)

    m_ref[...] = jnp.full_like(m_ref, -jnp.inf)
    acc_ref[...] = jnp.zeros_like(acc_ref)
    p_ref[1] = jnp.zeros_like(p_ref[1])
    a_ref[1] = jnp.ones_like(a_ref[1])

    first_diag = qi * per_q
    scores(0, 0)

    def pair(u, carry):
        j = 2 * u
        values(jnp.maximum(j - 1, 0), 1)
        softmax(0, None)
        scores(j + 1, 1)
        values(j, 0)
        softmax(1, None)
        scores(j + 2, 0)
        return carry

    assert per_q == 2 and tq == ROW_TILE
    lax.fori_loop(0, qi, pair, 0)
    values(jnp.maximum(first_diag - 1, 0), 1)
    softmax(0, 0)
    scores(first_diag + 1, 1)
    values(first_diag, 0)
    softmax(1, 1)
    values(first_diag + 1, 1)

    lam = lam_ref[0, 0]
    o = (acc_ref[0, :HEAD_W] / acc_ref[0, HEAD_W:HEAD_W + 1]
         - lam * (acc_ref[1, :HEAD_W] / acc_ref[1, HEAD_W:HEAD_W + 1]))
    o = o * lax.rsqrt(jnp.mean(o * o, axis=0, keepdims=True) + RMS_EPS) * g_ref[...] * out_scale
    o_ref[...] = o.T.astype(o_ref.dtype)


def _attention(qt, k, vt, lam, subln_g, bsz, seq, out_scale):
    n, d = k.shape
    tq, tk = ATT_Q_TILE, ATT_K_TILE
    nq, nk = seq // tq, seq // tk
    k3 = k.reshape(bsz, seq, d)
    return pl.pallas_call(
        functools.partial(_attn_kernel, out_scale=out_scale),
        grid=(bsz, DA_HEADS, nq),
        in_specs=[pl.BlockSpec(memory_space=pltpu.SMEM),
                  pl.BlockSpec((1, 1, 1, HEAD_W, tq), lambda b, h, i: (b, h, i, 0, 0)),
                  pl.BlockSpec((1, seq, HEAD_W), lambda b, h, i: (b, 0, h)),
                  pl.BlockSpec((1, 1, nk, HEAD_W + V_PAD, tk), lambda b, h, i: (b, h, 0, 0, 0)),
                  pl.BlockSpec((HEAD_W, 1), lambda b, h, i: (0, 0))],
        out_specs=pl.BlockSpec((tq, HEAD_W), lambda b, h, i: (b * nq + i, h)),
        out_shape=jax.ShapeDtypeStruct((n, d), BF16),
        scratch_shapes=[pltpu.VMEM((2, 1, tq), F32),
                        pltpu.VMEM((2, HEAD_W + V_PAD, tq), F32),
                        pltpu.VMEM((2, 2, tk, tq), F32), pltpu.VMEM((2, 2, tk, tq), BF16),
                        pltpu.VMEM((2, 2, 1, tq), F32)],
        compiler_params=_params(3),
        name="diff_attention",
    )(lam.reshape(1, 1), qt, k3, vt, subln_g[:, None])


def _gmlp_kernel(x_ref, wi_ref, bi_ref, ng_ref, nb_ref, ws_ref, bs_ref, wo_ref, bo_ref,
                 lg_ref, lb_ref, o_ref, u_ref, v_ref, vn_ref, *, alpha):
    rows, half = u_ref.shape
    gw = half // GMLP_GROUPS
    xb = x_ref[...].astype(BF16)
    for c in range(0, 2 * half, COL_CHUNK):
        z = jnp.dot(xb, wi_ref[:, c:c + COL_CHUNK], preferred_element_type=F32) + bi_ref[:, c:c + COL_CHUNK]
        z = 0.5 * z * (1.0 + lax.erf(z * math.sqrt(0.5)))
        if c < half:
            u_ref[:, c:c + COL_CHUNK] = z
        else:
            v_ref[:, c - half:c - half + COL_CHUNK] = z
    for r0 in range(0, rows, CHUNK):
        vn_ref[r0:r0 + CHUNK] = _layer_norm(v_ref[r0:r0 + CHUNK], ng_ref[...], nb_ref[...]).astype(BF16)
    for g in range(GMLP_GROUPS):
        cs = slice(g * gw, (g + 1) * gw)
        sv = jnp.dot(ws_ref[g], vn_ref[:, cs], preferred_element_type=F32) + bs_ref[g]
        vn_ref[:, cs] = (u_ref[:, cs] * sv).astype(BF16)
    y = jnp.dot(vn_ref[...], wo_ref[...], preferred_element_type=F32) + bo_ref[...]
    o_ref[...] = _layer_norm(alpha * x_ref[...] + y, lg_ref[...], lb_ref[...])


def _spatial_weights(w_s, b_s, t):
    pos = _tile_order_positions(t)
    chunk, p = pos // CHUNK, pos % CHUNK
    onehot = (p[:, None] == jnp.arange(CHUNK)[None, :]).astype(w_s.dtype)
    w = jnp.einsum("ip,gpq,jq->gij", onehot, w_s.astype(BF16).astype(w_s.dtype), onehot)
    keep = (chunk[:, None] == chunk[None, :]) & (p[None, :] <= p[:, None])
    w = jnp.where(keep[None], w, 0.0).astype(BF16)
    b = jnp.einsum("ip,gp->gi", onehot, b_s, precision=lax.Precision.HIGHEST)
    return w, b[:, :, None]


def _gmlp(x, w_in, b_in, n_g, n_b, w_s, b_s, w_out, b_out, ln_g, ln_b, alpha):
    n, d = x.shape
    f = w_in.shape[1]
    half = f // 2
    t = ROW_TILE
    ws, bs = _spatial_weights(w_s, b_s, t)
    row_spec = pl.BlockSpec((t, d), lambda i: (i, 0))
    return pl.pallas_call(
        functools.partial(_gmlp_kernel, alpha=alpha),
        grid=(n // t,),
        in_specs=[row_spec, _const_spec((d, f)), _const_spec((1, f)),
                  _const_spec((1, half)), _const_spec((1, half)),
                  _const_spec((GMLP_GROUPS, t, t)), _const_spec((GMLP_GROUPS, t, 1)),
                  _const_spec((half, d))] + [_const_spec((1, d))] * 3,
        out_specs=row_spec,
        out_shape=jax.ShapeDtypeStruct((n, d), F32),
        scratch_shapes=[pltpu.VMEM((t, half), F32), pltpu.VMEM((t, half), F32),
                        pltpu.VMEM((t, half), BF16)],
        compiler_params=_params(1),
        name="gmlp",
    )(x, w_in.astype(BF16), b_in[None], n_g[None], n_b[None], ws, bs, w_out.astype(BF16),
      b_out[None], ln_g[None], ln_b[None])


def kernel(x, a_w_in, a_b_in, a_w_dw, a_b_dw, a_ln_g, a_ln_b, a_w_out, a_b_out, b_w_qkv, b_lq1, b_lk1, b_lq2, b_lk2, b_subln_g, b_w_o, c_w_in, c_b_in, c_ln_g, c_ln_b, c_w_s, c_b_s, c_w_out, c_b_out, f_w_up, f_b_up, f_w_dw, f_b_dw, f_w_down, f_b_down, ln_mix_g, ln_mix_b, ln_ffn_g, ln_ffn_b):
    bsz, seq, d = x.shape
    depth = f_w_up.shape[0]
    alpha = (2 * depth) ** 0.25
    x = _to_tile_order(x.reshape(bsz * seq, d), ROW_TILE)
    for i in range(depth):
        kind, j = i % N_MIXERS, i // N_MIXERS
        mix_ln = (ln_mix_g[i], ln_mix_b[i], alpha)
        if kind == 0:
            h = _glu(x, a_w_in[j], a_b_in[j])
            x = _conv_out(h, x, seq, a_w_dw[j], a_b_dw[j], a_ln_g[j], a_ln_b[j],
                          a_w_out[j], a_b_out[j], *mix_ln)
        elif kind == 1:
            lambda_init = 0.8 - 0.6 * math.exp(-0.3 * i)
            lam = (jnp.exp(jnp.sum(b_lq1[j] * b_lk1[j])) - jnp.exp(jnp.sum(b_lq2[j] * b_lk2[j]))
                   + lambda_init)
            q, k, v = _qkv(x, b_w_qkv[j], bsz, seq)
            o = _attention(q, k, v, lam, b_subln_g[j], bsz, seq, 1.0 - lambda_init)
            x = _proj_ln(o, x, b_w_o[j], jnp.zeros((d,), F32), *mix_ln)
        else:
            x = _gmlp(x, c_w_in[j], c_b_in[j], c_ln_g[j], c_ln_b[j], c_w_s[j], c_b_s[j],
                      c_w_out[j], c_b_out[j], *mix_ln)
        x = _conv_ffn(x, seq, f_w_up[i], f_b_up[i], f_w_dw[i], f_b_dw[i],
                      f_w_down[i], f_b_down[i], ln_ffn_g[i], ln_ffn_b[i], alpha)
    return _from_tile_order(x, ROW_TILE).reshape(bsz, seq, d)
```
